```python
import jax
import jax.numpy as jnp
from jax import lax
import numpy as np

D_MODEL = 1024
BATCH = 2
SEQ = 8192
DEPTH = 4
DEC_BATCH = 128
DEC_SEQ = 1
PAST_LEN = 2048
PAGE_SIZE = 128

N_MIXERS = 2
HEAD_DIM = 64
N_HEADS = D_MODEL // HEAD_DIM
D_FF = 4 * D_MODEL
Q_BLOCK = 128
N_FOX = (DEPTH + N_MIXERS - 1) // N_MIXERS
N_RWKV = DEPTH // N_MIXERS
N_VRES = max(N_RWKV - 1, 0)
FOX_IN = 3 * D_MODEL + N_HEADS + D_MODEL
LORA_W = 64
LORA_A = 64
LORA_V = 32
LORA_G = 160
ATTN_SCALE = HEAD_DIM ** -0.5
RMS_EPS = 1e-6
GN_EPS = 64e-5
L2_EPS = 1e-12

kernel_name = "fox_rwkv7_hybrid_decode_step"


def rmsnorm(x, g):
    xf = x.astype(jnp.float32)
    y = xf * lax.rsqrt(jnp.mean(xf * xf, axis=-1, keepdims=True) + RMS_EPS)
    return (y * g.astype(jnp.float32)).astype(x.dtype)


def sqrelu_mlp(h, w1, w2):
    u = jax.nn.relu(jnp.einsum('btd,df->btf', h, w1))
    return jnp.einsum('btf,fd->btd', u * u, w2)


def fox_project(h, w_in, b_f, q_g, k_g):
    b, t, _ = h.shape
    proj = jnp.einsum('btd,de->bte', h, w_in)
    q, k, v, f_logit, gate = jnp.split(
        proj, [D_MODEL, 2 * D_MODEL, 3 * D_MODEL, 3 * D_MODEL + N_HEADS], axis=-1)
    q = rmsnorm(q.reshape(b, t, N_HEADS, HEAD_DIM), q_g)
    k = rmsnorm(k.reshape(b, t, N_HEADS, HEAD_DIM), k_g)
    v = v.reshape(b, t, N_HEADS, HEAD_DIM)
    log_f = jax.nn.log_sigmoid((f_logit + b_f).astype(jnp.float32))
    return q, k, v, log_f, gate


def fox_output(o, gate, w_o):
    b, t = o.shape[:2]
    o = o.reshape(b, t, D_MODEL) * jax.nn.sigmoid(gate)
    return jnp.einsum('btd,de->bte', o, w_o)


def fox_prompt_attend(q, k, v, log_f):
    b, s = q.shape[:2]
    n_blk = s // Q_BLOCK
    cum = jnp.cumsum(log_f, axis=1)
    cum_k = cum.transpose(0, 2, 1)[:, :, None, :]
    k_pos = jnp.arange(s)
    q_blocks = q.reshape(b, n_blk, Q_BLOCK, N_HEADS, HEAD_DIM).transpose(1, 0, 2, 3, 4)
    c_blocks = cum.reshape(b, n_blk, Q_BLOCK, N_HEADS).transpose(1, 0, 3, 2)

    def one_block(args):
        blk, q_blk, c_blk = args
        scores = jnp.einsum('bqhd,bkhd->bhqk', q_blk, k).astype(jnp.float32) * ATTN_SCALE
        scores = scores + c_blk[..., None] - cum_k
        q_pos = blk * Q_BLOCK + jnp.arange(Q_BLOCK)
        causal = k_pos[None, :] <= q_pos[:, None]
        p = jax.nn.softmax(jnp.where(causal, scores, -jnp.inf), axis=-1).astype(v.dtype)
        return jnp.einsum('bhqk,bkhd->bqhd', p, v)

    out = lax.map(one_block, (jnp.arange(n_blk), q_blocks, c_blocks))
    return out.transpose(1, 0, 2, 3, 4).reshape(b, s, N_HEADS, HEAD_DIM)


def fox_sample_attend(q, k_new, v_new, log_f_new, k_pool, v_pool, logf_pool, page_table):
    db, t = q.shape[:2]
    k_past = k_pool[page_table].reshape(db, -1, N_HEADS, HEAD_DIM)
    v_past = v_pool[page_table].reshape(db, -1, N_HEADS, HEAD_DIM)
    lf_past = logf_pool[page_table].reshape(db, -1, N_HEADS).astype(jnp.float32)
    past = lf_past.shape[1]
    cum = jnp.cumsum(jnp.concatenate([lf_past, log_f_new], axis=1), axis=1)
    cum_q = cum[:, past:].transpose(0, 2, 1)[..., None]
    cum_k = cum.transpose(0, 2, 1)[:, :, None, :]
    s_past = jnp.einsum('bqhd,bkhd->bhqk', q, k_past).astype(jnp.float32)
    s_new = jnp.einsum('bqhd,bkhd->bhqk', q, k_new).astype(jnp.float32)
    causal = jnp.arange(t)[None, :] <= jnp.arange(t)[:, None]
    s_new = jnp.where(causal, s_new, -jnp.inf)
    scores = jnp.concatenate([s_past, s_new], axis=-1) * ATTN_SCALE + cum_q - cum_k
    p = jax.nn.softmax(scores, axis=-1).astype(v_new.dtype)
    return (jnp.einsum('bhqk,bkhd->bqhd', p[..., :past], v_past)
            + jnp.einsum('bhqk,bkhd->bqhd', p[..., past:], v_new))


def rwkv_scan(state0, r, decay, k, v, kk, a):
    def step(S, inp):
        r_t, w_t, k_t, v_t, kk_t, a_t = inp
        s_kk = jnp.einsum('bhvk,bhk->bhv', S, -kk_t)
        S = (S * w_t[:, :, None, :] + s_kk[..., None] * (kk_t * a_t)[:, :, None, :]
             + v_t[..., None] * k_t[:, :, None, :])
        return S, jnp.einsum('bhvk,bhk->bhv', S, r_t)

    xs = (jnp.moveaxis(r, 1, 0), jnp.moveaxis(decay, 1, 0), jnp.moveaxis(k, 1, 0),
          jnp.moveaxis(v, 1, 0), jnp.moveaxis(kk, 1, 0), jnp.moveaxis(a, 1, 0))
    S, ys = lax.scan(step, state0.astype(jnp.float32), xs)
    return jnp.moveaxis(ys, 0, 1), S


def rwkv_layer(h, h_prev, state0, v_first, mu, w_rkv, w0, w1, w2, a0, a1, a2, g1, g2,
               k_k, k_a, r_k, gn_w, gn_b, w_o, v_gate):
    b, t, _ = h.shape
    mixed = h[None] + (h_prev - h)[None] * mu[:, None, None, :]
    r, k, v = jnp.einsum('cbtd,cde->cbte', mixed[:3], w_rkv)
    xv, xw, xa, xg = mixed[2], mixed[3], mixed[4], mixed[5]
    if v_gate is None:
        v_first = v
    else:
        v0, v1, v2 = v_gate
        v = v + (v_first - v) * jax.nn.sigmoid(v0 + (xv @ v1) @ v2)
    w_log = -jax.nn.softplus(-(w0 + jnp.tanh(xw @ w1) @ w2).astype(jnp.float32)) - 0.5
    decay = jnp.exp(-jnp.exp(w_log))
    a = jax.nn.sigmoid((a0 + (xa @ a1) @ a2).astype(jnp.float32))
    g = jax.nn.sigmoid(xg @ g1) @ g2
    heads = lambda z: z.astype(jnp.float32).reshape(b, t, N_HEADS, HEAD_DIM)
    kk = heads(k * k_k)
    kk = kk / jnp.maximum(jnp.sqrt(jnp.sum(kk * kk, axis=-1, keepdims=True)), L2_EPS)
    k_mod = heads(k * (1.0 + (a - 1.0) * k_a))
    r_h, v_h, a_h = heads(r), heads(v), heads(a)
    y, S = rwkv_scan(state0, r_h, heads(decay), k_mod, v_h, kk, a_h)
    mean = jnp.mean(y, axis=-1, keepdims=True)
    var = jnp.mean(jnp.square(y - mean), axis=-1, keepdims=True)
    yn = ((y - mean) * lax.rsqrt(var + GN_EPS)).reshape(b, t, D_MODEL) * gn_w + gn_b
    bonus = jnp.sum(r_h * k_mod * r_k, axis=-1, keepdims=True) * v_h
    out = (yn + bonus.reshape(b, t, D_MODEL)) * g
    return jnp.einsum('btd,de->bte', out.astype(h.dtype), w_o), S, v_first


def setup_inputs(seed: int = 0) -> dict:
    key = jax.random.key(seed)
    keys = iter(jax.random.split(key, 48))

    def nrm(shape, scale):
        return scale * jax.random.normal(next(keys), shape, jnp.float32)

    n_pages = PAST_LEN // PAGE_SIZE
    n_used = DEC_BATCH * n_pages
    n_phys = n_used + n_used // 4
    sd = D_MODEL ** -0.5
    page_table = jax.random.permutation(next(keys), n_phys)[:n_used].reshape(
        DEC_BATCH, n_pages).astype(jnp.int32)
    return {
        "x_prompt": nrm((BATCH, SEQ, D_MODEL), 1.0),
        "x_sample": nrm((DEC_BATCH, DEC_SEQ, D_MODEL), 1.0),
        "cache_k": nrm((N_FOX, n_phys, PAGE_SIZE, N_HEADS, HEAD_DIM), 1.0),
        "cache_v": nrm((N_FOX, n_phys, PAGE_SIZE, N_HEADS, HEAD_DIM), 1.0),
        "cache_logf": jax.nn.log_sigmoid(2.0 + nrm((N_FOX, n_phys, PAGE_SIZE, N_HEADS), 1.0)),
        "page_table": page_table,
        "state_shift": nrm((N_RWKV, DEC_BATCH, D_MODEL), 1.0),
        "state_wkv": nrm((N_RWKV, DEC_BATCH, N_HEADS, HEAD_DIM, HEAD_DIM), 0.3),
        "norm_mix": 1.0 + nrm((DEPTH, D_MODEL), 0.02),
        "norm_mlp": 1.0 + nrm((DEPTH, D_MODEL), 0.02),
        "norm_out": 1.0 + nrm((D_MODEL,), 0.02),
        "fox_w_in": nrm((N_FOX, D_MODEL, FOX_IN), sd),
        "fox_b_f": 2.0 + nrm((N_FOX, N_HEADS), 0.1),
        "fox_q_norm": 1.0 + nrm((N_FOX, HEAD_DIM), 0.02),
        "fox_k_norm": 1.0 + nrm((N_FOX, HEAD_DIM), 0.02),
        "fox_w_o": nrm((N_FOX, D_MODEL, D_MODEL), sd),
        "rwkv_mu": jax.random.uniform(next(keys), (N_RWKV, 6, D_MODEL), jnp.float32),
        "rwkv_w_rkv": nrm((N_RWKV, 3, D_MODEL, D_MODEL), sd),
        "rwkv_w0": -2.5 + nrm((N_RWKV, D_MODEL), 1.0),
        "rwkv_w1": nrm((N_RWKV, D_MODEL, LORA_W), sd),
        "rwkv_w2": nrm((N_RWKV, LORA_W, D_MODEL), 0.1 * LORA_W ** -0.5),
        "rwkv_a0": nrm((N_RWKV, D_MODEL), 0.5),
        "rwkv_a1": nrm((N_RWKV, D_MODEL, LORA_A), sd),
        "rwkv_a2": nrm((N_RWKV, LORA_A, D_MODEL), 0.5 * LORA_A ** -0.5),
        "rwkv_v0": 1.0 + nrm((N_VRES, D_MODEL), 0.1),
        "rwkv_v1": nrm((N_VRES, D_MODEL, LORA_V), sd),
        "rwkv_v2": nrm((N_VRES, LORA_V, D_MODEL), 0.3 * LORA_V ** -0.5),
        "rwkv_g1": nrm((N_RWKV, D_MODEL, LORA_G), sd),
        "rwkv_g2": nrm((N_RWKV, LORA_G, D_MODEL), LORA_G ** -0.5),
        "rwkv_k_k": 0.85 + nrm((N_RWKV, D_MODEL), 0.05),
        "rwkv_k_a": 1.0 + nrm((N_RWKV, D_MODEL), 0.05),
        "rwkv_r_k": nrm((N_RWKV, N_HEADS, HEAD_DIM), 0.1),
        "rwkv_gn_w": 1.0 + nrm((N_RWKV, D_MODEL), 0.05),
        "rwkv_gn_b": nrm((N_RWKV, D_MODEL), 0.01),
        "rwkv_w_o": nrm((N_RWKV, D_MODEL, D_MODEL), sd),
        "mlp_w1": nrm((DEPTH, D_MODEL, D_FF), sd),
        "mlp_w2": nrm((DEPTH, D_FF, D_MODEL), D_FF ** -0.5),
    }


def reference(x_prompt, x_sample, cache_k, cache_v, cache_logf, page_table, state_shift, state_wkv,
              norm_mix, norm_mlp, norm_out, fox_w_in, fox_b_f, fox_q_norm, fox_k_norm, fox_w_o,
              rwkv_mu, rwkv_w_rkv, rwkv_w0, rwkv_w1, rwkv_w2, rwkv_a0, rwkv_a1, rwkv_a2,
              rwkv_v0, rwkv_v1, rwkv_v2, rwkv_g1, rwkv_g2, rwkv_k_k, rwkv_k_a, rwkv_r_k,
              rwkv_gn_w, rwkv_gn_b, rwkv_w_o, mlp_w1, mlp_w2):
    yp, ys = x_prompt, x_sample
    k_p, v_p, lf_p, k_s, v_s, lf_s = [], [], [], [], [], []
    sh_p, wkv_p, sh_s, wkv_s = [], [], [], []
    vf_p = vf_s = None
    for layer in range(DEPTH):
        i = layer // N_MIXERS
        hp = rmsnorm(yp, norm_mix[layer])
        hs = rmsnorm(ys, norm_mix[layer])
        if layer % N_MIXERS == 0:
            fw = (fox_w_in[i], fox_b_f[i], fox_q_norm[i], fox_k_norm[i])
            q, k, v, lf, gate = fox_project(hp, *fw)
            mix_p = fox_output(fox_prompt_attend(q, k, v, lf), gate, fox_w_o[i])
            k_p.append(k); v_p.append(v); lf_p.append(lf)
            q, k, v, lf, gate = fox_project(hs, *fw)
            o = fox_sample_attend(q, k, v, lf, cache_k[i], cache_v[i], cache_logf[i], page_table)
            mix_s = fox_output(o, gate, fox_w_o[i])
            k_s.append(k); v_s.append(v); lf_s.append(lf)
        else:
            rw = (rwkv_mu[i], rwkv_w_rkv[i], rwkv_w0[i], rwkv_w1[i], rwkv_w2[i], rwkv_a0[i],
                  rwkv_a1[i], rwkv_a2[i], rwkv_g1[i], rwkv_g2[i], rwkv_k_k[i], rwkv_k_a[i],
                  rwkv_r_k[i], rwkv_gn_w[i], rwkv_gn_b[i], rwkv_w_o[i])
            vg = None if i == 0 else (rwkv_v0[i - 1], rwkv_v1[i - 1], rwkv_v2[i - 1])
            hp_prev = jnp.concatenate([jnp.zeros_like(hp[:, :1]), hp[:, :-1]], axis=1)
            st0_p = jnp.zeros((hp.shape[0], N_HEADS, HEAD_DIM, HEAD_DIM), jnp.float32)
            mix_p, st_p, vf_p = rwkv_layer(hp, hp_prev, st0_p, vf_p, *rw, vg)
            hs_prev = jnp.concatenate([state_shift[i][:, None, :].astype(hs.dtype), hs[:, :-1]], axis=1)
            mix_s, st_s, vf_s = rwkv_layer(hs, hs_prev, state_wkv[i], vf_s, *rw, vg)
            sh_p.append(hp[:, -1]); wkv_p.append(st_p)
            sh_s.append(hs[:, -1]); wkv_s.append(st_s)
        yp = yp + mix_p.astype(yp.dtype)
        ys = ys + mix_s.astype(ys.dtype)
        yp = yp + sqrelu_mlp(rmsnorm(yp, norm_mlp[layer]), mlp_w1[layer], mlp_w2[layer]).astype(yp.dtype)
        ys = ys + sqrelu_mlp(rmsnorm(ys, norm_mlp[layer]), mlp_w1[layer], mlp_w2[layer]).astype(ys.dtype)
    y_prompt = rmsnorm(yp, norm_out)
    y_sample = rmsnorm(ys, norm_out)
    return (y_prompt, y_sample,
            jnp.stack(k_p), jnp.stack(v_p), jnp.stack(lf_p),
            jnp.stack(k_s), jnp.stack(v_s), jnp.stack(lf_s),
            jnp.stack(sh_p), jnp.stack(wkv_p), jnp.stack(sh_s), jnp.stack(wkv_s))
```

```python
import functools

import jax
import jax.numpy as jnp
from jax import lax
from jax.experimental import pallas as pl
from jax.experimental.pallas import tpu as pltpu

F32 = jnp.float32
BF = jnp.bfloat16

D_MODEL = 1024
HEAD_DIM = 64
N_HEADS = D_MODEL // HEAD_DIM
N_PAIRS = N_HEADS // 2
LANES = 128
D_FF = 4 * D_MODEL
PAGE = 128
ATTN_SCALE = HEAD_DIM ** -0.5
RMS_EPS = 1e-6
GN_EPS = 64e-5
L2_EPS = 1e-12
NEG = -1e30

TM = 256
TQ = 512
TCUM = 512
CHUNK = 64
TB = 256
FF_CHUNK = 1024
VMEM_LIMIT = 56 * 1024 * 1024


def _cparams(n_axes):
    return pltpu.CompilerParams(dimension_semantics=("arbitrary",) * n_axes,
                                vmem_limit_bytes=VMEM_LIMIT)


def _const_spec(shape):
    nd = len(shape)
    return pl.BlockSpec(shape, lambda *_: (0,) * nd, pipeline_mode=pl.Buffered(1))


def _dot(a, b):
    return jnp.dot(a.astype(BF), b.astype(BF), preferred_element_type=F32)


def _dot_nt(a, b):
    return lax.dot_general(a.astype(BF), b.astype(BF), (((1,), (1,)), ((), ())),
                           preferred_element_type=F32)


def _split2(x):
    hi = x.astype(BF)
    lo = (x - hi.astype(F32)).astype(BF)
    return hi, lo


def _split3(x):
    hi = x.astype(BF)
    r1 = x - hi.astype(F32)
    mid = r1.astype(BF)
    lo = (r1 - mid.astype(F32)).astype(BF)
    return hi, mid, lo


def _dot_exact_rhs(x, sel):
    hi, lo = _split2(x)
    return (jnp.dot(hi, sel, preferred_element_type=F32)
            + jnp.dot(lo, sel, preferred_element_type=F32))


def _dot_exact_lhs(sel, x):
    hi, lo = _split2(x)
    return (jnp.dot(sel, hi, preferred_element_type=F32)
            + jnp.dot(sel, lo, preferred_element_type=F32))


def _rms(x, g):
    return x * lax.rsqrt(jnp.mean(x * x, axis=-1, keepdims=True) + RMS_EPS) * g


def _log_sigmoid(x):
    return jnp.minimum(x, 0.0) - jnp.log(1.0 + jnp.exp(-jnp.abs(x)))


def _sigmoid(x):
    return 1.0 / (1.0 + jnp.exp(-x))


def _fox_in_kernel(y_ref, g_ref, wqkv_ref, wf_ref, wg_ref, bf_ref, qg_ref, kg_ref, e1_ref, e2_ref,
                   q_ref, k_ref, kb_ref, v_ref, vb_ref, lf_ref, sg_ref):
    h = _rms(y_ref[...], g_ref[...])
    hb = h.astype(BF)
    e1 = e1_ref[...]
    e2 = e2_ref[...]

    def headnorm(z, gain):
        ms = _dot_exact_rhs(z * z, e1) * (1.0 / HEAD_DIM)
        inv = lax.rsqrt(ms + RMS_EPS)
        return z * _dot_exact_rhs(inv, e2) * gain

    q = jnp.dot(hb, wqkv_ref[:, 0:D_MODEL], preferred_element_type=F32)
    q_ref[...] = (headnorm(q, qg_ref[...]) * ATTN_SCALE).astype(BF)
    k = jnp.dot(hb, wqkv_ref[:, D_MODEL:2 * D_MODEL], preferred_element_type=F32)
    k = headnorm(k, kg_ref[...])
    k_ref[...] = k
    kb_ref[...] = k.astype(BF)
    v = jnp.dot(hb, wqkv_ref[:, 2 * D_MODEL:3 * D_MODEL], preferred_element_type=F32)
    v_ref[...] = v
    vb_ref[...] = v.astype(BF)
    h_hi, h_lo = _split2(h)
    wf_hi = wf_ref[0]
    wf_lo = wf_ref[1]
    f_logit = (jnp.dot(h_hi, wf_hi, preferred_element_type=F32)
               + jnp.dot(h_lo, wf_hi, preferred_element_type=F32)
               + jnp.dot(h_hi, wf_lo, preferred_element_type=F32))
    lane = lax.broadcasted_iota(jnp.int32, f_logit.shape, 1)
    lf_ref[...] = jnp.where(lane < N_HEADS, _log_sigmoid(f_logit + bf_ref[...]), 0.0)
    gate = jnp.dot(hb, wg_ref[...], preferred_element_type=F32)
    sg_ref[...] = _sigmoid(gate)


def _fox_in(y, g, wqkv, wf, wg, b_f, q_g, k_g, e1, e2, tm):
    m = y.shape[0]
    row = lambda w: pl.BlockSpec((tm, w), lambda i: (i, 0))
    outs = [jax.ShapeDtypeStruct((m, D_MODEL), BF),
            jax.ShapeDtypeStruct((m, D_MODEL), F32),
            jax.ShapeDtypeStruct((m, D_MODEL), BF),
            jax.ShapeDtypeStruct((m, D_MODEL), F32),
            jax.ShapeDtypeStruct((m, D_MODEL), BF),
            jax.ShapeDtypeStruct((m, LANES), F32),
            jax.ShapeDtypeStruct((m, D_MODEL), F32)]
    return pl.pallas_call(
        _fox_in_kernel,
        grid=(m // tm,),
        in_specs=[row(D_MODEL), _const_spec((1, D_MODEL)), _const_spec(wqkv.shape),
                  _const_spec(wf.shape), _const_spec(wg.shape), _const_spec((1, LANES)),
                  _const_spec((1, D_MODEL)), _const_spec((1, D_MODEL)),
                  _const_spec(e1.shape), _const_spec(e2.shape)],
        out_specs=[row(D_MODEL)] * 5 + [row(LANES), row(D_MODEL)],
        out_shape=outs,
        compiler_params=_cparams(1),
        name="fox_in",
    )(y, g, wqkv, wf, wg, b_f, q_g, k_g, e1, e2)


def _fox_cum_kernel(lf_ref, tri_ref, pq_ref, pk_ref, cq_ref, ck_ref, qa_ref, ka_ref, carry_ref):
    @pl.when(pl.program_id(1) == 0)
    def _():
        carry_ref[...] = jnp.zeros_like(carry_ref)

    tri = tri_ref[...]
    x = lf_ref[0]
    c = carry_ref[0:1, :]
    for part in _split3(x):
        c = c + jnp.dot(tri, part, preferred_element_type=F32)
    carry_ref[0:1, :] = c[TCUM - 1:TCUM, :]
    qa = cq_ref[...]
    ka = ck_ref[...]
    for j, part in enumerate(_split3(c)):
        qa = qa + jnp.dot(part, pq_ref[j], preferred_element_type=F32)
        ka = ka + jnp.dot(part, pk_ref[j], preferred_element_type=F32)
    qa_ref[0] = qa.astype(BF)
    ka_ref[0] = ka.astype(BF)


def _fox_cum(lf, tri, pq, pk, cq, ck):
    b, t, _ = lf.shape
    blk = lambda w: pl.BlockSpec((1, TCUM, w), lambda i, j: (i, j, 0))
    return pl.pallas_call(
        _fox_cum_kernel,
        grid=(b, t // TCUM),
        in_specs=[blk(LANES), _const_spec(tri.shape), _const_spec(pq.shape), _const_spec(pk.shape),
                  _const_spec(cq.shape), _const_spec(ck.shape)],
        out_specs=[blk(D_MODEL), blk(D_MODEL)],
        out_shape=[jax.ShapeDtypeStruct((b, t, D_MODEL), BF)] * 2,
        scratch_shapes=[pltpu.VMEM((8, LANES), F32)],
        compiler_params=_cparams(2),
        name="fox_cum",
    )(lf, tri, pq, pk, cq, ck)


def _flash_kernel(q_ref, qa_ref, k_ref, ka_ref, v_ref, sg_ref, o_ref, kf_ref, m_ref, l_ref, acc_ref):
    qi = pl.program_id(2)

    @pl.when(qi == 0)
    def _():
        kf_ref[:, 0:LANES] = k_ref[0]
        kf_ref[:, LANES:2 * LANES] = ka_ref[0]

    lane2 = lax.broadcasted_iota(jnp.int32, (1, 2 * LANES), 1)
    first = (lane2 % LANES) < HEAD_DIM
    qf = jnp.concatenate([q_ref[0], qa_ref[0]], axis=1)
    zero = jnp.zeros_like(qf)
    qh = (jnp.where(first, qf, zero), jnp.where(first, zero, qf))
    m_ref[...] = jnp.full(m_ref.shape, NEG, F32)
    l_ref[...] = jnp.zeros(l_ref.shape, F32)
    acc_ref[...] = jnp.zeros(acc_ref.shape, F32)
    row = lax.broadcasted_iota(jnp.int32, (TQ, TQ), 0)
    col = lax.broadcasted_iota(jnp.int32, (TQ, TQ), 1)
    causal = col <= row

    def step(kb, diagonal):
        start = pl.multiple_of(kb * TQ, TQ)
        kblk = kf_ref[pl.ds(start, TQ), :]
        vblk = v_ref[0, pl.ds(start, TQ), :]
        for j in range(2):
            s = lax.dot_general(qh[j], kblk, (((1,), (1,)), ((), ())), preferred_element_type=F32)
            if diagonal:
                s = jnp.where(causal, s, NEG)
            m_prev = m_ref[j]
            m_next = jnp.maximum(m_prev, jnp.max(s, axis=1, keepdims=True))
            p = jnp.exp(s - m_next[:, 0:1])
            alpha = jnp.exp(m_prev - m_next)
            l_ref[j] = alpha * l_ref[j] + jnp.sum(p, axis=1, keepdims=True)
            acc_ref[j] = alpha * acc_ref[j] + jnp.dot(p.astype(BF), vblk, preferred_element_type=F32)
            m_ref[j] = m_next

    def body(kb, carry):
        step(kb, False)
        return carry

    lax.fori_loop(0, qi, body, 0)
    step(qi, True)
    lane = lax.broadcasted_iota(jnp.int32, (1, LANES), 1)
    o = jnp.where(lane < HEAD_DIM, acc_ref[0] / l_ref[0], acc_ref[1] / l_ref[1])
    o_ref[0] = (o * sg_ref[0]).astype(BF)


def _flash(q, qa, kb, ka, vb, sg):
    b, t, _ = q.shape
    qblk = pl.BlockSpec((1, TQ, LANES), lambda i, p, j: (i, j, p))
    kblk = pl.BlockSpec((1, t, LANES), lambda i, p, j: (i, 0, p))
    return pl.pallas_call(
        _flash_kernel,
        grid=(b, N_PAIRS, t // TQ),
        in_specs=[qblk, qblk, kblk, kblk, kblk, qblk],
        out_specs=qblk,
        out_shape=jax.ShapeDtypeStruct((b, t, D_MODEL), BF),
        scratch_shapes=[pltpu.VMEM((t, 2 * LANES), BF),
                        pltpu.VMEM((2, TQ, LANES), F32),
                        pltpu.VMEM((2, TQ, LANES), F32),
                        pltpu.VMEM((2, TQ, LANES), F32)],
        compiler_params=_cparams(3),
        name="fox_flash",
    )(q, qa, kb, ka, vb, sg)


def _decode_kernel(pt_ref, q_ref, kn_ref, vn_ref, lfn_ref, sg_ref, kp_ref, vp_ref, lfp_ref, su_ref,
                   o_ref, m_ref, l_ref, acc_ref, tail_ref):
    del pt_ref
    j = pl.program_id(1)
    q = q_ref[0]

    @pl.when(j == 0)
    def _():
        m_ref[...] = jnp.sum(q * kn_ref[0], axis=-1, keepdims=True) + jnp.zeros(m_ref.shape, F32)
        l_ref[...] = jnp.ones(l_ref.shape, F32)
        acc_ref[...] = vn_ref[0]
        tail_ref[...] = lfn_ref[0] + jnp.zeros(tail_ref.shape, F32)

    lf = lfp_ref[0, 0]
    bias = tail_ref[...] + _dot_exact_rhs(lf, su_ref[...])
    tail_ref[...] = tail_ref[...] + jnp.sum(lf, axis=1, keepdims=True)
    for h in range(N_HEADS):
        k_h = kp_ref[0, 0, :, h, :]
        v_h = vp_ref[0, 0, :, h, :]
        q_h = jnp.broadcast_to(q[h:h + 1, :], (8, HEAD_DIM))
        s = _dot_nt(q_h, k_h)[0:1, :] + bias[h:h + 1, :]
        m_prev = m_ref[h:h + 1, :]
        m_next = jnp.maximum(m_prev, jnp.max(s, axis=1, keepdims=True))
        p = jnp.exp(s - m_next)
        alpha = jnp.exp(m_prev - m_next)
        l_ref[h:h + 1, :] = alpha * l_ref[h:h + 1, :] + jnp.sum(p, axis=1, keepdims=True)
        pv = _dot(jnp.broadcast_to(p, (8, PAGE)), v_h)[0:1, :]
        acc_ref[h:h + 1, :] = alpha[:, 0:HEAD_DIM] * acc_ref[h:h + 1, :] + pv
        m_ref[h:h + 1, :] = m_next

    @pl.when(j == pl.num_programs(1) - 1)
    def _():
        o_ref[0] = acc_ref[...] / l_ref[:, 0:HEAD_DIM] * sg_ref[0]


def _decode(page_table, layer, q, k_new, v_new, lf_new, sg, cache_k, cache_v, logf_t, su):
    n_s, n_pages = page_table.shape
    tok = pl.BlockSpec((1, N_HEADS, HEAD_DIM), lambda i, j, pt: (i, 0, 0))
    page = pl.BlockSpec((1, 1, PAGE, N_HEADS, HEAD_DIM),
                        lambda i, j, pt: (layer, pt[i, n_pages - 1 - j], 0, 0, 0))
    lfpage = pl.BlockSpec((1, 1, N_HEADS, PAGE), lambda i, j, pt: (layer, pt[i, n_pages - 1 - j], 0, 0))
    grid_spec = pltpu.PrefetchScalarGridSpec(
        num_scalar_prefetch=1,
        grid=(n_s, n_pages),
        in_specs=[tok, tok, tok, pl.BlockSpec((1, N_HEADS, 1), lambda i, j, pt: (i, 0, 0)), tok,
                  page, page, lfpage, pl.BlockSpec(su.shape, lambda i, j, pt: (0, 0))],
        out_specs=tok,
        scratch_shapes=[pltpu.VMEM((N_HEADS, LANES), F32), pltpu.VMEM((N_HEADS, LANES), F32),
                        pltpu.VMEM((N_HEADS, HEAD_DIM), F32), pltpu.VMEM((N_HEADS, LANES), F32)])
    return pl.pallas_call(
        _decode_kernel,
        grid_spec=grid_spec,
        out_shape=jax.ShapeDtypeStruct((n_s, N_HEADS, HEAD_DIM), F32),
        compiler_params=_cparams(2),
        name="fox_decode",
    )(page_table, q, k_new, v_new, lf_new, sg, cache_k, cache_v, logf_t, su)


def _out_mlp_kernel(y_ref, a_ref, wo_ref, g_ref, w1_ref, w2_ref, go_ref, o_ref, *, final_norm):
    y = y_ref[...] + jnp.dot(a_ref[...].astype(BF), wo_ref[...], preferred_element_type=F32)
    hb = _rms(y, g_ref[...]).astype(BF)
    acc = y
    for c in range(D_FF // FF_CHUNK):
        u = jnp.dot(hb, w1_ref[:, c * FF_CHUNK:(c + 1) * FF_CHUNK], preferred_element_type=F32)
        u = jnp.maximum(u, 0.0)
        acc = acc + jnp.dot((u * u).astype(BF), w2_ref[c * FF_CHUNK:(c + 1) * FF_CHUNK, :],
                            preferred_element_type=F32)
    if final_norm:
        acc = _rms(acc, go_ref[...])
    o_ref[...] = acc


def _out_mlp(y, a, wo, g, w1, w2, g_out, final_norm, tm):
    m = y.shape[0]
    row = pl.BlockSpec((tm, D_MODEL), lambda i: (i, 0))
    return pl.pallas_call(
        functools.partial(_out_mlp_kernel, final_norm=final_norm),
        grid=(m // tm,),
        in_specs=[row, row, _const_spec(wo.shape), _const_spec((1, D_MODEL)), _const_spec(w1.shape),
                  _const_spec(w2.shape), _const_spec((1, D_MODEL))],
        out_specs=row,
        out_shape=jax.ShapeDtypeStruct((m, D_MODEL), F32),
        compiler_params=_cparams(1),
        name="out_mlp",
    )(y, a, wo, g, w1, w2, g_out)


def _rwkv_in_kernel(*refs, prev_normed, v_gate, tail):
    if v_gate:
        (y_ref, yp_ref, g_ref, mu_ref, wrkv_ref, w0_ref, w1_ref, w2_ref, a0_ref, a1_ref, a2_ref,
         g1_ref, g2_ref, kk_ref, ka_ref, e1_ref, e2_ref, v0_ref, v1_ref, v2_ref, vf_ref,
         r_out, lw_out, k_out, v_out, kap_out, b_out, g_out, h_out) = refs
    else:
        (y_ref, yp_ref, g_ref, mu_ref, wrkv_ref, w0_ref, w1_ref, w2_ref, a0_ref, a1_ref, a2_ref,
         g1_ref, g2_ref, kk_ref, ka_ref, e1_ref, e2_ref,
         r_out, lw_out, k_out, v_out, kap_out, b_out, g_out, h_out) = refs
    h = _rms(y_ref[...], g_ref[...])
    hp = yp_ref[...] if prev_normed else _rms(yp_ref[...], g_ref[...])
    d = hp - h
    tm = h.shape[0]
    h_out[0] = h[tm - tail:tm, :]
    mix = lambda c: h + d * mu_ref[c:c + 1, :]

    r_out[...] = _dot(mix(0), wrkv_ref[0])
    k = _dot(mix(1), wrkv_ref[1])
    xv = mix(2)
    v = _dot(xv, wrkv_ref[2])
    if v_gate:
        vg = _sigmoid(v0_ref[...] + _dot(_dot(xv, v1_ref[...]), v2_ref[...]))
        v = v + (vf_ref[...] - v) * vg
    v_out[...] = v
    w_pre = w0_ref[...] + _dot(jnp.tanh(_dot(mix(3), w1_ref[...])), w2_ref[...])
    lw_out[...] = -jnp.exp(_log_sigmoid(w_pre) - 0.5)
    a = _sigmoid(a0_ref[...] + _dot(_dot(mix(4), a1_ref[...]), a2_ref[...]))
    g_out[...] = _dot(_sigmoid(_dot(mix(5), g1_ref[...])), g2_ref[...])
    kk = k * kk_ref[...]
    ss = _dot_exact_rhs(kk * kk, e1_ref[...])
    inv = 1.0 / jnp.maximum(jnp.sqrt(ss), L2_EPS)
    kap = kk * _dot_exact_rhs(inv, e2_ref[...])
    kap_out[...] = kap
    b_out[...] = kap * a
    k_out[...] = k * (1.0 + (a - 1.0) * ka_ref[...])


def _rwkv_in(y, yprev, g, p, e1, e2, vgate, v_first, prev_normed, tm, tail):
    m = y.shape[0]
    row = pl.BlockSpec((tm, D_MODEL), lambda i: (i, 0))
    vec = _const_spec((1, D_MODEL))
    args = [y, yprev, g, p["mu"], p["w_rkv"], p["w0"], p["w1"], p["w2"], p["a0"], p["a1"], p["a2"],
            p["g1"], p["g2"], p["k_k"], p["k_a"], e1, e2]
    specs = [row, row, vec, _const_spec(p["mu"].shape), _const_spec(p["w_rkv"].shape), vec,
             _const_spec(p["w1"].shape), _const_spec(p["w2"].shape), vec, _const_spec(p["a1"].shape),
             _const_spec(p["a2"].shape), _const_spec(p["g1"].shape), _const_spec(p["g2"].shape),
             vec, vec, _const_spec(e1.shape), _const_spec(e2.shape)]
    if vgate is not None:
        args += [vgate["v0"], vgate["v1"], vgate["v2"], v_first]
        specs += [vec, _const_spec(vgate["v1"].shape), _const_spec(vgate["v2"].shape), row]
    n_tiles = m // tm
    outs = [jax.ShapeDtypeStruct((m, D_MODEL), F32)] * 7 + [jax.ShapeDtypeStruct((n_tiles, tail, D_MODEL), F32)]
    return pl.pallas_call(
        functools.partial(_rwkv_in_kernel, prev_normed=prev_normed, v_gate=vgate is not None, tail=tail),
        grid=(n_tiles,),
        in_specs=specs,
        out_specs=[row] * 7 + [pl.BlockSpec((1, tail, D_MODEL), lambda i: (i, 0, 0))],
        out_shape=outs,
        compiler_params=_cparams(1),
        name="rwkv_in",
    )(*args)


def _unit_lower_inverse(a_b, eye, dmask):
    ad = a_b * dmask
    low = a_b - ad
    a2 = _dot(ad, ad)
    a4 = _dot(a2, a2)
    a8 = _dot(a4, a4)
    x = eye - ad
    x = x + _dot(x, a2)
    x = x + _dot(x, a4)
    dinv = x + _dot(x, a8)
    n = _dot(dinv, low)
    n2 = _dot(n, n)
    y = dinv + _dot(n2, dinv)
    return y - _dot(n, y)


def _group_norm_gate(y, r, k, v, g, gnw, gnb, rk, eblk):
    mean = _dot_exact_rhs(y, eblk)
    dev = y - mean
    var = _dot_exact_rhs(dev * dev, eblk)
    yn = dev * lax.rsqrt(var + GN_EPS) * gnw + gnb
    bonus = _dot_exact_rhs(r * k * rk, eblk) * float(HEAD_DIM) * v
    return (yn + bonus) * g


def _rwkv_scan_kernel(r_ref, lw_ref, k_ref, v_ref, kap_ref, b_ref, g_ref, gnw_ref, gnb_ref, rk_ref,
                      tril_ref, amask_ref, dmask_ref, eye_ref, eblk_ref, a_out, st_out, st_ref):
    tb = pl.program_id(2)

    @pl.when(tb == 0)
    def _():
        st_ref[...] = jnp.zeros_like(st_ref)

    lane = lax.broadcasted_iota(jnp.int32, (1, LANES), 1)
    m0 = (lane < HEAD_DIM).astype(F32)
    m1 = 1.0 - m0
    stack = lambda x: jnp.concatenate([x * m0, x * m1], axis=0)
    tril = tril_ref[...]
    amask = amask_ref[...]
    dmask = dmask_ref[...]
    eye = eye_ref[...]
    c = CHUNK
    for ci in range(TB // CHUNK):
        rows = slice(ci * c, (ci + 1) * c)
        lw = lw_ref[0, rows, :]
        r = r_ref[0, rows, :]
        k = k_ref[0, rows, :]
        v = v_ref[0, rows, :]
        kap = kap_ref[0, rows, :]
        bb = b_ref[0, rows, :]
        cl = _dot_exact_lhs(tril, lw)
        cmid = cl[c // 2 - 1:c // 2, :]
        e_r = jnp.exp(cl - cmid)
        e_k = jnp.exp(cmid - cl)
        e_ka = jnp.exp(cl - lw - cmid)
        dc0 = jnp.exp(cmid)
        dc1 = jnp.exp(cl[c - 1:c, :] - cmid)
        gam = jnp.exp(cl[c - 1:c, :])
        r_t = r * e_r
        ka_t = kap * e_ka
        k_t = k * e_k
        b_t = bb * e_k
        r_true = stack(r_t * dc0)
        ka_true = stack(ka_t * dc0)
        k_hat = stack(k_t * dc1)
        b_hat = stack(b_t * dc1)
        v_bd = stack(v)
        pmat = jnp.concatenate([stack(ka_t), stack(r_t)], axis=0)
        qmat = jnp.concatenate([stack(k_t), stack(b_t)], axis=0)
        amat = _dot_nt(pmat, qmat) * amask
        a_k = amat[0:2 * c, 0:2 * c]
        a_b = amat[0:2 * c, 2 * c:4 * c]
        a_rk = amat[2 * c:4 * c, 0:2 * c]
        a_rb = amat[2 * c:4 * c, 2 * c:4 * c]
        tinv = _unit_lower_inverse(a_b, eye, dmask)
        akv = _dot(a_k, v_bd)
        wu = _dot(tinv, jnp.concatenate([ka_true, akv], axis=1))
        w = wu[:, 0:2 * c]
        u0 = wu[:, 2 * c:4 * c]
        st0 = st_ref[...]
        wr = _dot_nt(jnp.concatenate([w, r_true], axis=0), st0)
        u = wr[0:2 * c, :] + u0
        ut = _dot_nt(st0, w) + u0.T
        upd = _dot(jnp.concatenate([v_bd.T, -ut], axis=1), jnp.concatenate([k_hat, b_hat], axis=0))
        st_ref[...] = st0 * gam + upd
        y_bd = wr[2 * c:4 * c, :] + _dot(a_rk, v_bd) - _dot(a_rb, u)
        y = y_bd[0:c, :] + y_bd[c:2 * c, :]
        a_out[0, rows, :] = _group_norm_gate(y, r, k, v, g_ref[0, rows, :], gnw_ref[...], gnb_ref[...],
                                             rk_ref[...], eblk_ref[...]).astype(BF)

    @pl.when(tb == pl.num_programs(2) - 1)
    def _():
        st_out[0, 0] = st_ref[...]


def _rwkv_scan(r, lw, k, v, kap, bb, g, gnw, gnb, rk, consts):
    b, t, _ = r.shape
    blk = pl.BlockSpec((1, TB, LANES), lambda i, p, j: (i, j, p))
    vec = pl.BlockSpec((1, LANES), lambda i, p, j: (0, p))
    tril, amask, dmask, eye, eblk = consts
    return pl.pallas_call(
        _rwkv_scan_kernel,
        grid=(b, N_PAIRS, t // TB),
        in_specs=[blk] * 7 + [vec] * 3 + [_const_spec(x.shape) for x in consts],
        out_specs=[blk, pl.BlockSpec((1, 1, LANES, LANES), lambda i, p, j: (i, p, 0, 0))],
        out_shape=[jax.ShapeDtypeStruct((b, t, D_MODEL), BF),
                   jax.ShapeDtypeStruct((b, N_PAIRS, LANES, LANES), F32)],
        scratch_shapes=[pltpu.VMEM((LANES, LANES), F32)],
        compiler_params=_cparams(3),
        name="rwkv_scan",
    )(r, lw, k, v, kap, bb, g, gnw, gnb, rk, tril, amask, dmask, eye, eblk)


def _rwkv_step_kernel(s_ref, r_ref, lw_ref, k_ref, v_ref, kap_ref, b_ref, g_ref, gnw_ref, gnb_ref,
                      rk_ref, eye_ref, a_out, s_out):
    s = s_ref[0]
    eye = eye_ref[...][None]
    bc = lambda x_ref: x_ref[0][:, None, :]
    s_kk = -jnp.sum(s * bc(kap_ref), axis=-1, keepdims=True)
    v_col = jnp.sum(eye * bc(v_ref), axis=-1, keepdims=True)
    s_new = s * jnp.exp(bc(lw_ref)) + s_kk * bc(b_ref) + v_col * bc(k_ref)
    s_out[0] = s_new
    y_col = jnp.sum(s_new * bc(r_ref), axis=-1, keepdims=True)
    y = jnp.sum(eye * y_col, axis=1)
    r = r_ref[0]
    k = k_ref[0]
    v = v_ref[0]
    mean = jnp.mean(y, axis=-1, keepdims=True)
    dev = y - mean
    var = jnp.mean(dev * dev, axis=-1, keepdims=True)
    yn = dev * lax.rsqrt(var + GN_EPS) * gnw_ref[...] + gnb_ref[...]
    bonus = jnp.sum(r * k * rk_ref[...], axis=-1, keepdims=True) * v
    a_out[0] = (yn + bonus) * g_ref[0]


def _rwkv_step(state, r, lw, k, v, kap, bb, g, gnw, gnb, rk, eye):
    n_s = state.shape[0]
    tok = pl.BlockSpec((1, N_HEADS, HEAD_DIM), lambda i: (i, 0, 0))
    st = pl.BlockSpec((1, N_HEADS, HEAD_DIM, HEAD_DIM), lambda i: (i, 0, 0, 0))
    par = pl.BlockSpec((N_HEADS, HEAD_DIM), lambda i: (0, 0))
    return pl.pallas_call(
        _rwkv_step_kernel,
        grid=(n_s,),
        in_specs=[st] + [tok] * 7 + [par] * 3 + [pl.BlockSpec((HEAD_DIM, HEAD_DIM), lambda i: (0, 0))],
        out_specs=[tok, st],
        out_shape=[jax.ShapeDtypeStruct((n_s, N_HEADS, HEAD_DIM), F32),
                   jax.ShapeDtypeStruct(state.shape, F32)],
        compiler_params=_cparams(1),
        name="rwkv_step",
    )(state, r, lw, k, v, kap, bb, g, gnw, gnb, rk, eye)


def _head_selectors():
    d = jnp.arange(D_MODEL)
    e1 = (d[:, None] // HEAD_DIM == jnp.arange(LANES)[None, :]).astype(BF)
    return e1, e1.T


def _aug_placement():
    h = jnp.arange(LANES)
    col = jnp.arange(D_MODEL)
    base = (h // 2) * LANES + (h % 2) * HEAD_DIM
    valid = (h < N_HEADS)[:, None]
    place = lambda off: (valid & (col[None, :] == (base + off)[:, None])).astype(BF)
    pq = jnp.stack([place(0), place(1), place(2)])
    pk = jnp.stack([place(3), place(4), place(5)])
    within = col % HEAD_DIM
    cq = jnp.where((within >= 3) & (within < 6), -1.0, 0.0).astype(F32)[None, :]
    ck = jnp.where(within < 3, 1.0, 0.0).astype(F32)[None, :]
    return pq, pk, cq, ck


def _scan_constants():
    c = CHUNK
    i = jnp.arange(c)
    tril = (i[None, :] <= i[:, None]).astype(BF)
    r4 = jnp.arange(4 * c)
    same = (r4[:, None] // c) % 2 == (r4[None, :] // c) % 2
    tr = r4[:, None] % c
    ti = r4[None, :] % c
    strict = tr > ti
    incl = tr >= ti
    amask = (same & jnp.where((r4 < 2 * c)[:, None], strict, incl)).astype(F32)
    r2 = jnp.arange(2 * c)
    dmask = (r2[:, None] // 16 == r2[None, :] // 16).astype(F32)
    eye = jnp.eye(2 * c, dtype=F32)
    eblk = ((r2[:, None] // HEAD_DIM == r2[None, :] // HEAD_DIM).astype(F32) / HEAD_DIM).astype(BF)
    return tril, amask, dmask, eye, eblk


def _pad_cols(w, n):
    return jnp.pad(w, ((0, 0), (0, n - w.shape[1])))


def _pad_rows(w, n):
    return jnp.pad(w, ((0, n - w.shape[0]), (0, 0)))


def _heads(x):
    return x.reshape(x.shape[0], N_HEADS, HEAD_DIM)


def kernel(x_prompt, x_sample, cache_k, cache_v, cache_logf, page_table, state_shift, state_wkv,
           norm_mix, norm_mlp, norm_out, fox_w_in, fox_b_f, fox_q_norm, fox_k_norm, fox_w_o,
           rwkv_mu, rwkv_w_rkv, rwkv_w0, rwkv_w1, rwkv_w2, rwkv_a0, rwkv_a1, rwkv_a2,
           rwkv_v0, rwkv_v1, rwkv_v2, rwkv_g1, rwkv_g2, rwkv_k_k, rwkv_k_a, rwkv_r_k,
           rwkv_gn_w, rwkv_gn_b, rwkv_w_o, mlp_w1, mlp_w2):
    b, t, _ = x_prompt.shape
    n_s = x_sample.shape[0]
    depth = norm_mix.shape[0]
    yp = x_prompt.reshape(b * t, D_MODEL)
    ys = x_sample.reshape(n_s, D_MODEL)
    vec = lambda x: x.reshape(1, -1).astype(F32)
    e1, e2 = _head_selectors()
    pq, pk, cq, ck = _aug_placement()
    scan_consts = _scan_constants()
    tri_cum = (jnp.arange(TCUM)[None, :] <= jnp.arange(TCUM)[:, None]).astype(BF)
    tok = jnp.arange(PAGE)
    suffix = (tok[:, None] > tok[None, :]).astype(BF)
    eye64 = jnp.eye(HEAD_DIM, dtype=F32)
    logf_t = jnp.swapaxes(cache_logf, 2, 3)

    k_p, v_p, lf_p, k_s, v_s, lf_s = [], [], [], [], [], []
    sh_p, wkv_p, sh_s, wkv_s = [], [], [], []
    vf_p = vf_s = None
    for layer in range(depth):
        i = layer // 2
        g_mix = vec(norm_mix[layer])
        if layer % 2 == 0:
            w_in = fox_w_in[i]
            wqkv = w_in[:, :3 * D_MODEL].astype(BF)
            wf_full = _pad_cols(w_in[:, 3 * D_MODEL:3 * D_MODEL + N_HEADS], LANES)
            wf = jnp.stack(_split2(wf_full))
            wg = w_in[:, 3 * D_MODEL + N_HEADS:].astype(BF)
            b_f = _pad_cols(vec(fox_b_f[i]), LANES)
            q_g = vec(jnp.tile(fox_q_norm[i], N_HEADS))
            k_g = vec(jnp.tile(fox_k_norm[i], N_HEADS))
            fox_args = (g_mix, wqkv, wf, wg, b_f, q_g, k_g, e1, e2)
            q, k, kb, v, vb, lf, sg = _fox_in(yp, *fox_args, tm=TM)
            qa, ka = _fox_cum(lf.reshape(b, t, LANES), tri_cum, pq, pk, cq, ck)
            shp = lambda x: x.reshape(b, t, D_MODEL)
            mix_p = _flash(shp(q), qa, shp(kb), ka, shp(vb), shp(sg)).reshape(b * t, D_MODEL)
            k_p.append(k.reshape(b, t, N_HEADS, HEAD_DIM))
            v_p.append(v.reshape(b, t, N_HEADS, HEAD_DIM))
            lf_p.append(lf[:, :N_HEADS].reshape(b, t, N_HEADS))
            q, k, kb, v, vb, lf, sg = _fox_in(ys, *fox_args, tm=n_s)
            lf16 = lf[:, :N_HEADS]
            mix_s = _decode(page_table, i, _heads(q.astype(F32)), _heads(k), _heads(v),
                            lf16[:, :, None], _heads(sg), cache_k, cache_v, logf_t, suffix)
            mix_s = mix_s.reshape(n_s, D_MODEL)
            k_s.append(k.reshape(n_s, 1, N_HEADS, HEAD_DIM))
            v_s.append(v.reshape(n_s, 1, N_HEADS, HEAD_DIM))
            lf_s.append(lf16.reshape(n_s, 1, N_HEADS))
            w_o = fox_w_o[i].astype(BF)
        else:
            p = {
                "mu": _pad_rows(rwkv_mu[i], 8),
                "w_rkv": rwkv_w_rkv[i].astype(BF),
                "w0": vec(rwkv_w0[i]), "a0": vec(rwkv_a0[i]),
                "w1": _pad_cols(rwkv_w1[i], LANES).astype(BF), "w2": _pad_rows(rwkv_w2[i], LANES).astype(BF),
                "a1": _pad_cols(rwkv_a1[i], LANES).astype(BF), "a2": _pad_rows(rwkv_a2[i], LANES).astype(BF),
                "g1": _pad_cols(rwkv_g1[i], 2 * LANES).astype(BF), "g2": _pad_rows(rwkv_g2[i], 2 * LANES).astype(BF),
                "k_k": vec(rwkv_k_k[i]), "k_a": vec(rwkv_k_a[i]),
            }
            vgate = None
            if i > 0:
                vgate = {"v0": vec(rwkv_v0[i - 1]),
                         "v1": _pad_cols(rwkv_v1[i - 1], LANES).astype(BF),
                         "v2": _pad_rows(rwkv_v2[i - 1], LANES).astype(BF)}
            gnw, gnb, rk = vec(rwkv_gn_w[i]), vec(rwkv_gn_b[i]), vec(rwkv_r_k[i])
            yp3 = yp.reshape(b, t, D_MODEL)
            yprev = jnp.concatenate([jnp.zeros_like(yp3[:, :1]), yp3[:, :-1]], axis=1).reshape(b * t, D_MODEL)
            r, lw, k, v, kap, bb, g, htail = _rwkv_in(yp, yprev, g_mix, p, e1, e2, vgate, vf_p,
                                                      prev_normed=False, tm=TM, tail=8)
            if i == 0:
                vf_p = v
            shp = lambda x: x.reshape(b, t, D_MODEL)
            mix_p, st = _rwkv_scan(shp(r), shp(lw), shp(k), shp(v), shp(kap), shp(bb), shp(g),
                                   gnw, gnb, rk, scan_consts)
            mix_p = mix_p.reshape(b * t, D_MODEL)
            sh_p.append(htail.reshape(b, t // TM, 8, D_MODEL)[:, -1, -1, :])
            st = st.reshape(b, N_PAIRS, 2, HEAD_DIM, 2, HEAD_DIM)
            wkv_p.append(jnp.stack([st[:, :, 0, :, 0, :], st[:, :, 1, :, 1, :]], axis=2)
                         .reshape(b, N_HEADS, HEAD_DIM, HEAD_DIM))
            r, lw, k, v, kap, bb, g, htail = _rwkv_in(ys, state_shift[i], g_mix, p, e1, e2, vgate, vf_s,
                                                      prev_normed=True, tm=n_s, tail=n_s)
            if i == 0:
                vf_s = v
            hd = lambda x: x.reshape(N_HEADS, HEAD_DIM)
            mix_s, st_s = _rwkv_step(state_wkv[i], _heads(r), _heads(lw), _heads(k), _heads(v), _heads(kap),
                                     _heads(bb), _heads(g), hd(gnw), hd(gnb), hd(rk), eye64)
            mix_s = mix_s.reshape(n_s, D_MODEL)
            sh_s.append(htail.reshape(n_s, D_MODEL))
            wkv_s.append(st_s)
            w_o = rwkv_w_o[i].astype(BF)
        last = layer == depth - 1
        mlp_args = (w_o, vec(norm_mlp[layer]), mlp_w1[layer].astype(BF), mlp_w2[layer].astype(BF),
                    vec(norm_out), last)
        yp = _out_mlp(yp, mix_p, *mlp_args, tm=TM)
        ys = _out_mlp(ys, mix_s, *mlp_args, tm=n_s)
    return (yp.reshape(b, t, D_MODEL), ys.reshape(n_s, 1, D_MODEL),
            jnp.stack(k_p), jnp.stack(v_p), jnp.stack(lf_p),
            jnp.stack(k_s), jnp.stack(v_s), jnp.stack(lf_s),
            jnp.stack(sh_p), jnp.stack(wkv_p), jnp.stack(sh_s), jnp.stack(wkv_s))
```

```python
import functools

import jax
import jax.numpy as jnp
from jax import lax
from jax.experimental import pallas as pl
from jax.experimental.pallas import tpu as pltpu

F32 = jnp.float32
BF = jnp.bfloat16

D_MODEL = 1024
HEAD_DIM = 64
N_HEADS = D_MODEL // HEAD_DIM
N_PAIRS = N_HEADS // 2
LANES = 128
D_FF = 4 * D_MODEL
PAGE = 128
ATTN_SCALE = HEAD_DIM ** -0.5
RMS_EPS = 1e-6
GN_EPS = 64e-5
L2_EPS = 1e-12
NEG = -1e30

TM = 256
TQ = 512
TCUM = 512
CHUNK = 64
TB = 512
SLAB = 64
GROUP = 256
FF_CHUNK = 1024
VMEM_LIMIT = 56 * 1024 * 1024


def _cparams(n_axes):
    return pltpu.CompilerParams(dimension_semantics=("arbitrary",) * n_axes,
                                vmem_limit_bytes=VMEM_LIMIT)


def _const_spec(shape):
    nd = len(shape)
    return pl.BlockSpec(shape, lambda *_: (0,) * nd, pipeline_mode=pl.Buffered(1))


def _dot(a, b):
    return jnp.dot(a.astype(BF), b.astype(BF), preferred_element_type=F32)


def _dot_nt(a, b):
    return lax.dot_general(a.astype(BF), b.astype(BF), (((1,), (1,)), ((), ())),
                           preferred_element_type=F32)


def _bmm(a, b):
    return lax.dot_general(a.astype(BF), b.astype(BF), (((2,), (1,)), ((0,), (0,))),
                           preferred_element_type=F32)


def _bmm_nt(a, b):
    return lax.dot_general(a.astype(BF), b.astype(BF), (((2,), (2,)), ((0,), (0,))),
                           preferred_element_type=F32)


def _bmm_tn(a, b):
    return lax.dot_general(a.astype(BF), b.astype(BF), (((1,), (1,)), ((0,), (0,))),
                           preferred_element_type=F32)


def _split2(x):
    hi = x.astype(BF)
    lo = (x - hi.astype(F32)).astype(BF)
    return hi, lo


def _split3(x):
    hi = x.astype(BF)
    r1 = x - hi.astype(F32)
    mid = r1.astype(BF)
    lo = (r1 - mid.astype(F32)).astype(BF)
    return hi, mid, lo


def _dot_exact_rhs(x, sel):
    hi, lo = _split2(x)
    return (jnp.dot(hi, sel, preferred_element_type=F32)
            + jnp.dot(lo, sel, preferred_element_type=F32))


def _dot_exact_lhs(sel, x):
    hi, lo = _split2(x)
    return (jnp.dot(sel, hi, preferred_element_type=F32)
            + jnp.dot(sel, lo, preferred_element_type=F32))


def _rms(x, g):
    return x * lax.rsqrt(jnp.mean(x * x, axis=-1, keepdims=True) + RMS_EPS) * g


def _log_sigmoid(x):
    return jnp.minimum(x, 0.0) - jnp.log(1.0 + jnp.exp(-jnp.abs(x)))


def _sigmoid(x):
    return 1.0 / (1.0 + jnp.exp(-x))


def _fox_in_kernel(y_ref, g_ref, wqkv_ref, wf_ref, wg_ref, bf_ref, qg_ref, kg_ref, e1_ref, e2_ref,
                   q_ref, k_ref, kb_ref, v_ref, vb_ref, lf_ref, sg_ref):
    h = _rms(y_ref[...], g_ref[...])
    hb = h.astype(BF)
    e1 = e1_ref[...]
    e2 = e2_ref[...]

    def headnorm(z, gain):
        ms = _dot_exact_rhs(z * z, e1) * (1.0 / HEAD_DIM)
        inv = lax.rsqrt(ms + RMS_EPS)
        return z * _dot_exact_rhs(inv, e2) * gain

    q = jnp.dot(hb, wqkv_ref[:, 0:D_MODEL], preferred_element_type=F32)
    q_ref[...] = (headnorm(q, qg_ref[...]) * ATTN_SCALE).astype(BF)
    k = jnp.dot(hb, wqkv_ref[:, D_MODEL:2 * D_MODEL], preferred_element_type=F32)
    k = headnorm(k, kg_ref[...])
    k_ref[...] = k
    kb_ref[...] = k.astype(BF)
    v = jnp.dot(hb, wqkv_ref[:, 2 * D_MODEL:3 * D_MODEL], preferred_element_type=F32)
    v_ref[...] = v
    vb_ref[...] = v.astype(BF)
    h_hi, h_lo = _split2(h)
    wf_hi = wf_ref[0]
    wf_lo = wf_ref[1]
    f_logit = (jnp.dot(h_hi, wf_hi, preferred_element_type=F32)
               + jnp.dot(h_lo, wf_hi, preferred_element_type=F32)
               + jnp.dot(h_hi, wf_lo, preferred_element_type=F32))
    lane = lax.broadcasted_iota(jnp.int32, f_logit.shape, 1)
    lf_ref[...] = jnp.where(lane < N_HEADS, _log_sigmoid(f_logit + bf_ref[...]), 0.0)
    gate = jnp.dot(hb, wg_ref[...], preferred_element_type=F32)
    sg_ref[...] = _sigmoid(gate)


def _fox_in(y, g, wqkv, wf, wg, b_f, q_g, k_g, e1, e2, tm):
    m = y.shape[0]
    row = lambda w: pl.BlockSpec((tm, w), lambda i: (i, 0))
    outs = [jax.ShapeDtypeStruct((m, D_MODEL), BF),
            jax.ShapeDtypeStruct((m, D_MODEL), F32),
            jax.ShapeDtypeStruct((m, D_MODEL), BF),
            jax.ShapeDtypeStruct((m, D_MODEL), F32),
            jax.ShapeDtypeStruct((m, D_MODEL), BF),
            jax.ShapeDtypeStruct((m, LANES), F32),
            jax.ShapeDtypeStruct((m, D_MODEL), F32)]
    return pl.pallas_call(
        _fox_in_kernel,
        grid=(m // tm,),
        in_specs=[row(D_MODEL), _const_spec((1, D_MODEL)), _const_spec(wqkv.shape),
                  _const_spec(wf.shape), _const_spec(wg.shape), _const_spec((1, LANES)),
                  _const_spec((1, D_MODEL)), _const_spec((1, D_MODEL)),
                  _const_spec(e1.shape), _const_spec(e2.shape)],
        out_specs=[row(D_MODEL)] * 5 + [row(LANES), row(D_MODEL)],
        out_shape=outs,
        compiler_params=_cparams(1),
        name="fox_in",
    )(y, g, wqkv, wf, wg, b_f, q_g, k_g, e1, e2)


def _fox_cum_kernel(lf_ref, tri_ref, pq_ref, pk_ref, cq_ref, ck_ref, qa_ref, ka_ref, carry_ref):
    @pl.when(pl.program_id(1) == 0)
    def _():
        carry_ref[...] = jnp.zeros_like(carry_ref)

    tri = tri_ref[...]
    x = lf_ref[0]
    c = carry_ref[0:1, :]
    for part in _split3(x):
        c = c + jnp.dot(tri, part, preferred_element_type=F32)
    carry_ref[0:1, :] = c[TCUM - 1:TCUM, :]
    qa = cq_ref[...]
    ka = ck_ref[...]
    for j, part in enumerate(_split3(c)):
        qa = qa + jnp.dot(part, pq_ref[j], preferred_element_type=F32)
        ka = ka + jnp.dot(part, pk_ref[j], preferred_element_type=F32)
    qa_ref[0] = qa.astype(BF)
    ka_ref[0] = ka.astype(BF)


def _fox_cum(lf, tri, pq, pk, cq, ck):
    b, t, _ = lf.shape
    blk = lambda w: pl.BlockSpec((1, TCUM, w), lambda i, j: (i, j, 0))
    return pl.pallas_call(
        _fox_cum_kernel,
        grid=(b, t // TCUM),
        in_specs=[blk(LANES), _const_spec(tri.shape), _const_spec(pq.shape), _const_spec(pk.shape),
                  _const_spec(cq.shape), _const_spec(ck.shape)],
        out_specs=[blk(D_MODEL), blk(D_MODEL)],
        out_shape=[jax.ShapeDtypeStruct((b, t, D_MODEL), BF)] * 2,
        scratch_shapes=[pltpu.VMEM((8, LANES), F32)],
        compiler_params=_cparams(2),
        name="fox_cum",
    )(lf, tri, pq, pk, cq, ck)


def _flash_kernel(q_ref, qa_ref, k_ref, ka_ref, v_ref, sg_ref, o_ref,
                  kf_ref, vx_ref, qh_ref, s_ref, m_ref, acc_ref):
    qi = pl.program_id(2)

    @pl.when(qi == 0)
    def _():
        kf_ref[:, 0:LANES] = k_ref[0]
        kf_ref[:, LANES:2 * LANES] = ka_ref[0]
        vx_ref[:, 0:LANES] = v_ref[0]
        vx_ref[:, LANES:2 * LANES] = jnp.ones((vx_ref.shape[0], LANES), BF)

    lane2 = lax.broadcasted_iota(jnp.int32, (1, 2 * LANES), 1)
    first = (lane2 % LANES) < HEAD_DIM
    qf = jnp.concatenate([q_ref[0], qa_ref[0]], axis=1)
    zero = jnp.zeros_like(qf)
    qh_ref[0:TQ, :] = jnp.where(first, qf, zero)
    qh_ref[TQ:2 * TQ, :] = jnp.where(first, zero, qf)
    m_ref[...] = jnp.full(m_ref.shape, NEG, F32)
    acc_ref[...] = jnp.zeros(acc_ref.shape, F32)
    groups = [(g * GROUP, (g + 1) * GROUP) for g in range(2 * TQ // GROUP)]

    def scores(kb, slot):
        kblk = kf_ref[pl.ds(pl.multiple_of(kb * TQ, TQ), TQ), :]
        for lo, hi in groups:
            s_ref[slot, lo:hi, :] = lax.dot_general(qh_ref[lo:hi, :], kblk, (((1,), (1,)), ((), ())),
                                                    preferred_element_type=F32)

    def softmax_pv(kb, slot, diagonal):
        vblk = vx_ref[pl.ds(pl.multiple_of(kb * TQ, TQ), TQ), :]
        for lo, hi in groups:
            p_parts, al_parts = [], []
            for r0 in range(lo, hi, SLAB):
                s = s_ref[slot, r0:r0 + SLAB, :]
                if diagonal:
                    row = (r0 % TQ) + lax.broadcasted_iota(jnp.int32, (SLAB, TQ), 0)
                    col = lax.broadcasted_iota(jnp.int32, (SLAB, TQ), 1)
                    s = jnp.where(col <= row, s, NEG)
                m_prev = m_ref[r0:r0 + SLAB, :]
                m_next = jnp.maximum(m_prev, jnp.max(s, axis=1, keepdims=True))
                p_parts.append(jnp.exp(s - jnp.concatenate([m_next] * (TQ // LANES), axis=1)).astype(BF))
                al_parts.append(jnp.exp(m_prev - m_next))
                m_ref[r0:r0 + SLAB, :] = m_next
            pv = jnp.dot(jnp.concatenate(p_parts, axis=0), vblk, preferred_element_type=F32)
            alpha = jnp.concatenate(al_parts, axis=0)
            acc_ref[lo:hi, :] = acc_ref[lo:hi, :] * jnp.concatenate([alpha, alpha], axis=1) + pv

    def body(kb, carry):
        slot = lax.rem(kb, 2)
        scores(kb + 1, 1 - slot)
        softmax_pv(kb, slot, False)
        return carry

    scores(0, 0)
    lax.fori_loop(0, qi, body, 0)
    softmax_pv(qi, lax.rem(qi, 2), True)
    lane = lax.broadcasted_iota(jnp.int32, (1, LANES), 1)
    o0 = acc_ref[0:TQ, 0:LANES] / acc_ref[0:TQ, LANES:2 * LANES]
    o1 = acc_ref[TQ:2 * TQ, 0:LANES] / acc_ref[TQ:2 * TQ, LANES:2 * LANES]
    o_ref[0] = (jnp.where(lane < HEAD_DIM, o0, o1) * sg_ref[0]).astype(BF)


def _flash(q, qa, kb, ka, vb, sg):
    b, t, _ = q.shape
    qblk = pl.BlockSpec((1, TQ, LANES), lambda i, p, j: (i, j, p))
    kblk = pl.BlockSpec((1, t, LANES), lambda i, p, j: (i, 0, p))
    return pl.pallas_call(
        _flash_kernel,
        grid=(b, N_PAIRS, t // TQ),
        in_specs=[qblk, qblk, kblk, kblk, kblk, qblk],
        out_specs=qblk,
        out_shape=jax.ShapeDtypeStruct((b, t, D_MODEL), BF),
        scratch_shapes=[pltpu.VMEM((t, 2 * LANES), BF),
                        pltpu.VMEM((t, 2 * LANES), BF),
                        pltpu.VMEM((2 * TQ, 2 * LANES), BF),
                        pltpu.VMEM((2, 2 * TQ, TQ), F32),
                        pltpu.VMEM((2 * TQ, LANES), F32),
                        pltpu.VMEM((2 * TQ, 2 * LANES), F32)],
        compiler_params=_cparams(3),
        name="fox_flash",
    )(q, qa, kb, ka, vb, sg)


def _decode_kernel(pt_ref, q_ref, kn_ref, vn_ref, lfn_ref, sg_ref, su_ref, eye_ref, *rest):
    del pt_ref
    n = (len(rest) - 3) // 3
    k_refs, v_refs, lf_refs = rest[0:n], rest[n:2 * n], rest[2 * n:3 * n]
    o_ref, s_ref, p_ref = rest[3 * n:]
    q = q_ref[0]
    eye = eye_ref[...]
    tail = lfn_ref[0]
    for j in reversed(range(n)):
        lf = lf_refs[j][0, 0]
        s_ref[j] = tail + _dot_exact_rhs(lf, su_ref[...])
        tail = tail + jnp.sum(lf, axis=1, keepdims=True)
    for h in range(N_HEADS):
        q_col = jnp.sum(eye * q[h:h + 1, :], axis=1, keepdims=True)
        for j in range(n):
            s_ref[j, h:h + 1, :] += jnp.sum(k_refs[j][0, 0, h] * q_col, axis=0, keepdims=True)
    s = s_ref[...]
    s_new = jnp.sum(q * kn_ref[0], axis=-1, keepdims=True)
    m = jnp.maximum(jnp.max(jnp.max(s, axis=0), axis=1, keepdims=True), s_new)
    p = jnp.exp(s - m[None])
    p_ref[...] = p
    p_new = jnp.exp(s_new - m)
    inv_l = 1.0 / (jnp.sum(jnp.sum(p, axis=0), axis=1, keepdims=True) + p_new)
    new_part = p_new * vn_ref[0]
    scale = inv_l * sg_ref[0]
    for h in range(N_HEADS):
        acc = p_ref[0, h:h + 1, :] * v_refs[0][0, 0, h]
        for j in range(1, n):
            acc = acc + p_ref[j, h:h + 1, :] * v_refs[j][0, 0, h]
        o_col = jnp.sum(acc, axis=1, keepdims=True)
        o_row = jnp.sum(eye * o_col, axis=0, keepdims=True)
        o_ref[0, h:h + 1, :] = (o_row + new_part[h:h + 1, :]) * scale[h:h + 1, :]


def _decode(page_table, layer, q, k_new, v_new, lf_new, sg, cache_kt, cache_vt, logf_t, su, eye):
    n_s, n_pages = page_table.shape
    tok = pl.BlockSpec((1, N_HEADS, HEAD_DIM), lambda i, pt: (i, 0, 0))
    page = lambda j: pl.BlockSpec((1, 1, N_HEADS, HEAD_DIM, PAGE), lambda i, pt: (layer, pt[i, j], 0, 0, 0))
    lfpage = lambda j: pl.BlockSpec((1, 1, N_HEADS, PAGE), lambda i, pt: (layer, pt[i, j], 0, 0))
    pages = list(range(n_pages))
    grid_spec = pltpu.PrefetchScalarGridSpec(
        num_scalar_prefetch=1,
        grid=(n_s,),
        in_specs=[tok, tok, tok, pl.BlockSpec((1, N_HEADS, 1), lambda i, pt: (i, 0, 0)), tok,
                  pl.BlockSpec(su.shape, lambda i, pt: (0, 0)), pl.BlockSpec(eye.shape, lambda i, pt: (0, 0))]
        + [page(j) for j in pages] + [page(j) for j in pages] + [lfpage(j) for j in pages],
        out_specs=tok,
        scratch_shapes=[pltpu.VMEM((n_pages, N_HEADS, PAGE), F32), pltpu.VMEM((n_pages, N_HEADS, PAGE), F32)])
    return pl.pallas_call(
        _decode_kernel,
        grid_spec=grid_spec,
        out_shape=jax.ShapeDtypeStruct((n_s, N_HEADS, HEAD_DIM), F32),
        compiler_params=_cparams(1),
        name="fox_decode",
    )(page_table, q, k_new, v_new, lf_new, sg, su, eye,
      *([cache_kt] * n_pages), *([cache_vt] * n_pages), *([logf_t] * n_pages))


def _out_mlp_kernel(y_ref, a_ref, wo_ref, g_ref, w1_ref, w2_ref, go_ref, o_ref, *, final_norm):
    y = y_ref[...] + jnp.dot(a_ref[...].astype(BF), wo_ref[...], preferred_element_type=F32)
    hb = _rms(y, g_ref[...]).astype(BF)
    acc = y
    for c in range(D_FF // FF_CHUNK):
        u = jnp.dot(hb, w1_ref[:, c * FF_CHUNK:(c + 1) * FF_CHUNK], preferred_element_type=F32)
        u = jnp.maximum(u, 0.0)
        acc = acc + jnp.dot((u * u).astype(BF), w2_ref[c * FF_CHUNK:(c + 1) * FF_CHUNK, :],
                            preferred_element_type=F32)
    if final_norm:
        acc = _rms(acc, go_ref[...])
    o_ref[...] = acc


def _out_mlp(y, a, wo, g, w1, w2, g_out, final_norm, tm):
    m = y.shape[0]
    row = pl.BlockSpec((tm, D_MODEL), lambda i: (i, 0))
    return pl.pallas_call(
        functools.partial(_out_mlp_kernel, final_norm=final_norm),
        grid=(m // tm,),
        in_specs=[row, row, _const_spec(wo.shape), _const_spec((1, D_MODEL)), _const_spec(w1.shape),
                  _const_spec(w2.shape), _const_spec((1, D_MODEL))],
        out_specs=row,
        out_shape=jax.ShapeDtypeStruct((m, D_MODEL), F32),
        compiler_params=_cparams(1),
        name="out_mlp",
    )(y, a, wo, g, w1, w2, g_out)


def _rwkv_in_kernel(*refs, prev_normed, v_gate, tail):
    if v_gate:
        (y_ref, yp_ref, g_ref, mu_ref, wrkv_ref, w0_ref, w1_ref, w2_ref, a0_ref, a1_ref, a2_ref,
         g1_ref, g2_ref, kk_ref, ka_ref, e1_ref, e2_ref, v0_ref, v1_ref, v2_ref, vf_ref,
         r_out, lw_out, k_out, v_out, kap_out, b_out, g_out, h_out) = refs
    else:
        (y_ref, yp_ref, g_ref, mu_ref, wrkv_ref, w0_ref, w1_ref, w2_ref, a0_ref, a1_ref, a2_ref,
         g1_ref, g2_ref, kk_ref, ka_ref, e1_ref, e2_ref,
         r_out, lw_out, k_out, v_out, kap_out, b_out, g_out, h_out) = refs
    h = _rms(y_ref[...], g_ref[...])
    hp = yp_ref[...] if prev_normed else _rms(yp_ref[...], g_ref[...])
    d = hp - h
    tm = h.shape[0]
    h_out[0] = h[tm - tail:tm, :]
    mix = lambda c: h + d * mu_ref[c:c + 1, :]

    r_out[...] = _dot(mix(0), wrkv_ref[0])
    k = _dot(mix(1), wrkv_ref[1])
    xv = mix(2)
    v = _dot(xv, wrkv_ref[2])
    if v_gate:
        vg = _sigmoid(v0_ref[...] + _dot(_dot(xv, v1_ref[...]), v2_ref[...]))
        v = v + (vf_ref[...] - v) * vg
    v_out[...] = v
    w_pre = w0_ref[...] + _dot(jnp.tanh(_dot(mix(3), w1_ref[...])), w2_ref[...])
    lw_out[...] = -jnp.exp(_log_sigmoid(w_pre) - 0.5)
    a = _sigmoid(a0_ref[...] + _dot(_dot(mix(4), a1_ref[...]), a2_ref[...]))
    g_out[...] = _dot(_sigmoid(_dot(mix(5), g1_ref[...])), g2_ref[...])
    kk = k * kk_ref[...]
    ss = _dot_exact_rhs(kk * kk, e1_ref[...])
    inv = 1.0 / jnp.maximum(jnp.sqrt(ss), L2_EPS)
    kap = kk * _dot_exact_rhs(inv, e2_ref[...])
    kap_out[...] = kap
    b_out[...] = kap * a
    k_out[...] = k * (1.0 + (a - 1.0) * ka_ref[...])


def _rwkv_in(y, yprev, g, p, e1, e2, vgate, v_first, prev_normed, tm, tail):
    m = y.shape[0]
    row = pl.BlockSpec((tm, D_MODEL), lambda i: (i, 0))
    vec = _const_spec((1, D_MODEL))
    args = [y, yprev, g, p["mu"], p["w_rkv"], p["w0"], p["w1"], p["w2"], p["a0"], p["a1"], p["a2"],
            p["g1"], p["g2"], p["k_k"], p["k_a"], e1, e2]
    specs = [row, row, vec, _const_spec(p["mu"].shape), _const_spec(p["w_rkv"].shape), vec,
             _const_spec(p["w1"].shape), _const_spec(p["w2"].shape), vec, _const_spec(p["a1"].shape),
             _const_spec(p["a2"].shape), _const_spec(p["g1"].shape), _const_spec(p["g2"].shape),
             vec, vec, _const_spec(e1.shape), _const_spec(e2.shape)]
    if vgate is not None:
        args += [vgate["v0"], vgate["v1"], vgate["v2"], v_first]
        specs += [vec, _const_spec(vgate["v1"].shape), _const_spec(vgate["v2"].shape), row]
    n_tiles = m // tm
    outs = [jax.ShapeDtypeStruct((m, D_MODEL), F32)] * 7 + [jax.ShapeDtypeStruct((n_tiles, tail, D_MODEL), F32)]
    return pl.pallas_call(
        functools.partial(_rwkv_in_kernel, prev_normed=prev_normed, v_gate=vgate is not None, tail=tail),
        grid=(n_tiles,),
        in_specs=specs,
        out_specs=[row] * 7 + [pl.BlockSpec((1, tail, D_MODEL), lambda i: (i, 0, 0))],
        out_shape=outs,
        compiler_params=_cparams(1),
        name="rwkv_in",
    )(*args)


def _unit_lower_inverse(a_b, eye, dmask):
    ad = a_b * dmask
    low = a_b - ad
    a2 = _bmm(ad, ad)
    a4 = _bmm(a2, a2)
    a8 = _bmm(a4, a4)
    x = eye - ad
    x = x + _bmm(x, a2)
    x = x + _bmm(x, a4)
    dinv = x + _bmm(x, a8)
    n = _bmm(dinv, low)
    n2 = _bmm(n, n)
    y = dinv + _bmm(n2, dinv)
    return y - _bmm(n, y)


def _group_norm_gate(y, r, k, v, g, gnw, gnb, rk, eblk):
    mean = _dot_exact_rhs(y, eblk)
    dev = y - mean
    var = _dot_exact_rhs(dev * dev, eblk)
    yn = dev * lax.rsqrt(var + GN_EPS) * gnw + gnb
    bonus = _dot_exact_rhs(r * k * rk, eblk) * float(HEAD_DIM) * v
    return (yn + bonus) * g


def _rwkv_scan_kernel(r_ref, lw_ref, k_ref, v_ref, kap_ref, b_ref, g_ref, gnw_ref, gnb_ref, rk_ref,
                      tril_ref, amask_ref, dmask_ref, eye_ref, eblk_ref, a_out, st_out, st_ref, y_ref):
    tb = pl.program_id(2)

    @pl.when(tb == 0)
    def _():
        st_ref[...] = jnp.zeros_like(st_ref)

    lane = lax.broadcasted_iota(jnp.int32, (1, 1, LANES), 2)
    m0 = (lane < HEAD_DIM).astype(F32)
    m1 = 1.0 - m0
    stack = lambda x: jnp.concatenate([x * m0, x * m1], axis=1)
    nc = TB // CHUNK
    c = CHUNK
    amask = amask_ref[...]
    lw = lw_ref[0]
    r = r_ref[0]
    k = k_ref[0]
    v = v_ref[0]
    kap = kap_ref[0]
    bb = b_ref[0]
    tril = jnp.broadcast_to(tril_ref[...], (nc, c, c))
    lw_hi, lw_lo = _split2(lw)
    cl = _bmm(tril, lw_hi) + _bmm(tril, lw_lo)
    cmid = cl[:, c // 2 - 1:c // 2, :]
    clast = cl[:, c - 1:c, :]
    e_r = jnp.exp(cl - cmid)
    e_k = jnp.exp(cmid - cl)
    e_ka = jnp.exp(cl - lw - cmid)
    dc0 = jnp.exp(cmid)
    dc1 = jnp.exp(clast - cmid)
    gam = jnp.exp(clast)
    r_t = r * e_r
    ka_t = kap * e_ka
    k_t = k * e_k
    b_t = bb * e_k
    r_true = stack(r_t * dc0)
    ka_true = stack(ka_t * dc0)
    k_hat = stack(k_t * dc1)
    b_hat = stack(b_t * dc1)
    v_bd = stack(v)
    pmat = jnp.concatenate([stack(ka_t), stack(r_t)], axis=1)
    qmat = jnp.concatenate([stack(k_t), stack(b_t)], axis=1)
    amat = _bmm_nt(pmat, qmat) * amask
    a_k = amat[:, 0:2 * c, 0:2 * c]
    a_b = amat[:, 0:2 * c, 2 * c:4 * c]
    a_rk = amat[:, 2 * c:4 * c, 0:2 * c]
    a_rb = amat[:, 2 * c:4 * c, 2 * c:4 * c]
    tinv = _unit_lower_inverse(a_b, eye_ref[...], dmask_ref[...])
    akv = _bmm(a_k, v_bd)
    wu = _bmm(tinv, jnp.concatenate([ka_true, akv], axis=2))
    w = wu[:, :, 0:2 * c]
    u0 = wu[:, :, 2 * c:4 * c]
    qy = _bmm(a_rb, wu)
    qe = r_true - qy[:, :, 0:2 * c]
    y0 = _bmm(a_rk, v_bd) - qy[:, :, 2 * c:4 * c]
    wtb = _bmm_tn(w, b_hat)
    hm = _bmm_tn(jnp.concatenate([v_bd, u0], axis=1), jnp.concatenate([k_hat, -b_hat], axis=1))

    st = st_ref[...]
    for ci in range(nc):
        y_bd = _dot_nt(qe[ci], st) + y0[ci]
        y_ref[ci] = y_bd[0:c, :] + y_bd[c:2 * c, :]
        st = st * gam[ci] - _dot(st, wtb[ci]) + hm[ci]
    st_ref[...] = st

    flat = lambda x: x.reshape(TB, LANES)
    a = _group_norm_gate(flat(y_ref[...]), flat(r), flat(k), flat(v), flat(g_ref[0]), gnw_ref[...],
                         gnb_ref[...], rk_ref[...], eblk_ref[...])
    a_out[0] = a.reshape(nc, c, LANES).astype(BF)

    @pl.when(tb == pl.num_programs(2) - 1)
    def _():
        st_out[0, 0] = st


def _rwkv_scan(r, lw, k, v, kap, bb, g, gnw, gnb, rk, consts):
    b, n_chunks = r.shape[0], r.shape[1]
    nc = TB // CHUNK
    blk = pl.BlockSpec((1, nc, CHUNK, LANES), lambda i, p, j: (i, j, 0, p))
    vec = pl.BlockSpec((1, LANES), lambda i, p, j: (0, p))
    tril, amask, dmask, eye, eblk = consts
    return pl.pallas_call(
        _rwkv_scan_kernel,
        grid=(b, N_PAIRS, n_chunks // nc),
        in_specs=[blk] * 7 + [vec] * 3 + [_const_spec(x.shape) for x in consts],
        out_specs=[blk, pl.BlockSpec((1, 1, LANES, LANES), lambda i, p, j: (i, p, 0, 0))],
        out_shape=[jax.ShapeDtypeStruct(r.shape, BF),
                   jax.ShapeDtypeStruct((b, N_PAIRS, LANES, LANES), F32)],
        scratch_shapes=[pltpu.VMEM((LANES, LANES), F32), pltpu.VMEM((nc, CHUNK, LANES), F32)],
        compiler_params=_cparams(3),
        name="rwkv_scan",
    )(r, lw, k, v, kap, bb, g, gnw, gnb, rk, tril, amask, dmask, eye, eblk)


def _rwkv_step_kernel(s_ref, r_ref, lw_ref, k_ref, v_ref, kap_ref, b_ref, g_ref, gnw_ref, gnb_ref,
                      rk_ref, eye_ref, a_out, s_out):
    s = s_ref[0]
    eye = eye_ref[...][None]
    bc = lambda x_ref: x_ref[0][:, None, :]
    s_kk = -jnp.sum(s * bc(kap_ref), axis=-1, keepdims=True)
    v_col = jnp.sum(eye * bc(v_ref), axis=-1, keepdims=True)
    s_new = s * jnp.exp(bc(lw_ref)) + s_kk * bc(b_ref) + v_col * bc(k_ref)
    s_out[0] = s_new
    y_col = jnp.sum(s_new * bc(r_ref), axis=-1, keepdims=True)
    y = jnp.sum(eye * y_col, axis=1)
    r = r_ref[0]
    k = k_ref[0]
    v = v_ref[0]
    mean = jnp.mean(y, axis=-1, keepdims=True)
    dev = y - mean
    var = jnp.mean(dev * dev, axis=-1, keepdims=True)
    yn = dev * lax.rsqrt(var + GN_EPS) * gnw_ref[...] + gnb_ref[...]
    bonus = jnp.sum(r * k * rk_ref[...], axis=-1, keepdims=True) * v
    a_out[0] = (yn + bonus) * g_ref[0]


def _rwkv_step(state, r, lw, k, v, kap, bb, g, gnw, gnb, rk, eye):
    n_s = state.shape[0]
    tok = pl.BlockSpec((1, N_HEADS, HEAD_DIM), lambda i: (i, 0, 0))
    st = pl.BlockSpec((1, N_HEADS, HEAD_DIM, HEAD_DIM), lambda i: (i, 0, 0, 0))
    par = pl.BlockSpec((N_HEADS, HEAD_DIM), lambda i: (0, 0))
    return pl.pallas_call(
        _rwkv_step_kernel,
        grid=(n_s,),
        in_specs=[st] + [tok] * 7 + [par] * 3 + [pl.BlockSpec((HEAD_DIM, HEAD_DIM), lambda i: (0, 0))],
        out_specs=[tok, st],
        out_shape=[jax.ShapeDtypeStruct((n_s, N_HEADS, HEAD_DIM), F32),
                   jax.ShapeDtypeStruct(state.shape, F32)],
        compiler_params=_cparams(1),
        name="rwkv_step",
    )(state, r, lw, k, v, kap, bb, g, gnw, gnb, rk, eye)


def _head_selectors():
    d = jnp.arange(D_MODEL)
    e1 = (d[:, None] // HEAD_DIM == jnp.arange(LANES)[None, :]).astype(BF)
    return e1, e1.T


def _aug_placement():
    h = jnp.arange(LANES)
    col = jnp.arange(D_MODEL)
    base = (h // 2) * LANES + (h % 2) * HEAD_DIM
    valid = (h < N_HEADS)[:, None]
    place = lambda off: (valid & (col[None, :] == (base + off)[:, None])).astype(BF)
    pq = jnp.stack([place(0), place(1), place(2)])
    pk = jnp.stack([place(3), place(4), place(5)])
    within = col % HEAD_DIM
    cq = jnp.where((within >= 3) & (within < 6), -1.0, 0.0).astype(F32)[None, :]
    ck = jnp.where(within < 3, 1.0, 0.0).astype(F32)[None, :]
    return pq, pk, cq, ck


def _scan_constants():
    c = CHUNK
    i = jnp.arange(c)
    tril = (i[None, :] <= i[:, None]).astype(BF)
    r4 = jnp.arange(4 * c)
    same = (r4[:, None] // c) % 2 == (r4[None, :] // c) % 2
    tr = r4[:, None] % c
    ti = r4[None, :] % c
    strict = tr > ti
    incl = tr >= ti
    amask = (same & jnp.where((r4 < 2 * c)[:, None], strict, incl)).astype(F32)
    r2 = jnp.arange(2 * c)
    dmask = (r2[:, None] // 16 == r2[None, :] // 16).astype(F32)
    eye = jnp.eye(2 * c, dtype=F32)
    eblk = ((r2[:, None] // HEAD_DIM == r2[None, :] // HEAD_DIM).astype(F32) / HEAD_DIM).astype(BF)
    return tril, amask, dmask, eye, eblk


def _pad_cols(w, n):
    return jnp.pad(w, ((0, 0), (0, n - w.shape[1])))


def _pad_rows(w, n):
    return jnp.pad(w, ((0, n - w.shape[0]), (0, 0)))


def _heads(x):
    return x.reshape(x.shape[0], N_HEADS, HEAD_DIM)


def kernel(x_prompt, x_sample, cache_k, cache_v, cache_logf, page_table, state_shift, state_wkv,
           norm_mix, norm_mlp, norm_out, fox_w_in, fox_b_f, fox_q_norm, fox_k_norm, fox_w_o,
           rwkv_mu, rwkv_w_rkv, rwkv_w0, rwkv_w1, rwkv_w2, rwkv_a0, rwkv_a1, rwkv_a2,
           rwkv_v0, rwkv_v1, rwkv_v2, rwkv_g1, rwkv_g2, rwkv_k_k, rwkv_k_a, rwkv_r_k,
           rwkv_gn_w, rwkv_gn_b, rwkv_w_o, mlp_w1, mlp_w2):
    b, t, _ = x_prompt.shape
    n_s = x_sample.shape[0]
    depth = norm_mix.shape[0]
    yp = x_prompt.reshape(b * t, D_MODEL)
    ys = x_sample.reshape(n_s, D_MODEL)
    vec = lambda x: x.reshape(1, -1).astype(F32)
    e1, e2 = _head_selectors()
    pq, pk, cq, ck = _aug_placement()
    scan_consts = _scan_constants()
    tri_cum = (jnp.arange(TCUM)[None, :] <= jnp.arange(TCUM)[:, None]).astype(BF)
    tok = jnp.arange(PAGE)
    suffix = (tok[:, None] > tok[None, :]).astype(BF)
    eye64 = jnp.eye(HEAD_DIM, dtype=F32)
    logf_t = jnp.swapaxes(cache_logf, 2, 3)
    cache_kt = jnp.transpose(cache_k, (0, 1, 3, 4, 2))
    cache_vt = jnp.transpose(cache_v, (0, 1, 3, 4, 2))

    k_p, v_p, lf_p, k_s, v_s, lf_s = [], [], [], [], [], []
    sh_p, wkv_p, sh_s, wkv_s = [], [], [], []
    vf_p = vf_s = None
    for layer in range(depth):
        i = layer // 2
        g_mix = vec(norm_mix[layer])
        if layer % 2 == 0:
            w_in = fox_w_in[i]
            wqkv = w_in[:, :3 * D_MODEL].astype(BF)
            wf_full = _pad_cols(w_in[:, 3 * D_MODEL:3 * D_MODEL + N_HEADS], LANES)
            wf = jnp.stack(_split2(wf_full))
            wg = w_in[:, 3 * D_MODEL + N_HEADS:].astype(BF)
            b_f = _pad_cols(vec(fox_b_f[i]), LANES)
            q_g = vec(jnp.tile(fox_q_norm[i], N_HEADS))
            k_g = vec(jnp.tile(fox_k_norm[i], N_HEADS))
            fox_args = (g_mix, wqkv, wf, wg, b_f, q_g, k_g, e1, e2)
            q, k, kb, v, vb, lf, sg = _fox_in(yp, *fox_args, tm=TM)
            qa, ka = _fox_cum(lf.reshape(b, t, LANES), tri_cum, pq, pk, cq, ck)
            shp = lambda x: x.reshape(b, t, D_MODEL)
            mix_p = _flash(shp(q), qa, shp(kb), ka, shp(vb), shp(sg)).reshape(b * t, D_MODEL)
            k_p.append(k.reshape(b, t, N_HEADS, HEAD_DIM))
            v_p.append(v.reshape(b, t, N_HEADS, HEAD_DIM))
            lf_p.append(lf[:, :N_HEADS].reshape(b, t, N_HEADS))
            q, k, kb, v, vb, lf, sg = _fox_in(ys, *fox_args, tm=n_s)
            lf16 = lf[:, :N_HEADS]
            mix_s = _decode(page_table, i, _heads(q.astype(F32)), _heads(k), _heads(v),
                            lf16[:, :, None], _heads(sg), cache_kt, cache_vt, logf_t, suffix, eye64)
            mix_s = mix_s.reshape(n_s, D_MODEL)
            k_s.append(k.reshape(n_s, 1, N_HEADS, HEAD_DIM))
            v_s.append(v.reshape(n_s, 1, N_HEADS, HEAD_DIM))
            lf_s.append(lf16.reshape(n_s, 1, N_HEADS))
            w_o = fox_w_o[i].astype(BF)
        else:
            p = {
                "mu": _pad_rows(rwkv_mu[i], 8),
                "w_rkv": rwkv_w_rkv[i].astype(BF),
                "w0": vec(rwkv_w0[i]), "a0": vec(rwkv_a0[i]),
                "w1": _pad_cols(rwkv_w1[i], LANES).astype(BF), "w2": _pad_rows(rwkv_w2[i], LANES).astype(BF),
                "a1": _pad_cols(rwkv_a1[i], LANES).astype(BF), "a2": _pad_rows(rwkv_a2[i], LANES).astype(BF),
                "g1": _pad_cols(rwkv_g1[i], 2 * LANES).astype(BF), "g2": _pad_rows(rwkv_g2[i], 2 * LANES).astype(BF),
                "k_k": vec(rwkv_k_k[i]), "k_a": vec(rwkv_k_a[i]),
            }
            vgate = None
            if i > 0:
                vgate = {"v0": vec(rwkv_v0[i - 1]),
                         "v1": _pad_cols(rwkv_v1[i - 1], LANES).astype(BF),
                         "v2": _pad_rows(rwkv_v2[i - 1], LANES).astype(BF)}
            gnw, gnb, rk = vec(rwkv_gn_w[i]), vec(rwkv_gn_b[i]), vec(rwkv_r_k[i])
            yp3 = yp.reshape(b, t, D_MODEL)
            yprev = jnp.concatenate([jnp.zeros_like(yp3[:, :1]), yp3[:, :-1]], axis=1).reshape(b * t, D_MODEL)
            r, lw, k, v, kap, bb, g, htail = _rwkv_in(yp, yprev, g_mix, p, e1, e2, vgate, vf_p,
                                                      prev_normed=False, tm=TM, tail=8)
            if i == 0:
                vf_p = v
            shp = lambda x: x.reshape(b, t // CHUNK, CHUNK, D_MODEL)
            mix_p, st = _rwkv_scan(shp(r), shp(lw), shp(k), shp(v), shp(kap), shp(bb), shp(g),
                                   gnw, gnb, rk, scan_consts)
            mix_p = mix_p.reshape(b * t, D_MODEL)
            sh_p.append(htail.reshape(b, t // TM, 8, D_MODEL)[:, -1, -1, :])
            st = st.reshape(b, N_PAIRS, 2, HEAD_DIM, 2, HEAD_DIM)
            wkv_p.append(jnp.stack([st[:, :, 0, :, 0, :], st[:, :, 1, :, 1, :]], axis=2)
                         .reshape(b, N_HEADS, HEAD_DIM, HEAD_DIM))
            r, lw, k, v, kap, bb, g, htail = _rwkv_in(ys, state_shift[i], g_mix, p, e1, e2, vgate, vf_s,
                                                      prev_normed=True, tm=n_s, tail=n_s)
            if i == 0:
                vf_s = v
            hd = lambda x: x.reshape(N_HEADS, HEAD_DIM)
            mix_s, st_s = _rwkv_step(state_wkv[i], _heads(r), _heads(lw), _heads(k), _heads(v), _heads(kap),
                                     _heads(bb), _heads(g), hd(gnw), hd(gnb), hd(rk), eye64)
            mix_s = mix_s.reshape(n_s, D_MODEL)
            sh_s.append(htail.reshape(n_s, D_MODEL))
            wkv_s.append(st_s)
            w_o = rwkv_w_o[i].astype(BF)
        last = layer == depth - 1
        mlp_args = (w_o, vec(norm_mlp[layer]), mlp_w1[layer].astype(BF), mlp_w2[layer].astype(BF),
                    vec(norm_out), last)
        yp = _out_mlp(yp, mix_p, *mlp_args, tm=TM)
        ys = _out_mlp(ys, mix_s, *mlp_args, tm=n_s)
    return (yp.reshape(b, t, D_MODEL), ys.reshape(n_s, 1, D_MODEL),
            jnp.stack(k_p), jnp.stack(v_p), jnp.stack(lf_p),
            jnp.stack(k_s), jnp.stack(v_s), jnp.stack(lf_s),
            jnp.stack(sh_p), jnp.stack(wkv_p), jnp.stack(sh_s), jnp.stack(wkv_s))
```

```python
import functools
import math

import jax
import jax.numpy as jnp
from jax import lax
from jax.experimental import pallas as pl
from jax.experimental.pallas import tpu as pltpu

F32 = jnp.float32
BF = jnp.bfloat16

D_MODEL = 1024
HEAD_DIM = 64
N_HEADS = D_MODEL // HEAD_DIM
N_PAIRS = N_HEADS // 2
LANES = 128
D_FF = 4 * D_MODEL
PAGE = 128
ATTN_SCALE = HEAD_DIM ** -0.5
LOG2E = math.log2(math.e)
RMS_EPS = 1e-6
GN_EPS = 64e-5
L2_EPS = 1e-12
NEG = -1e30

TM = 256
TQ = 512
TCUM = 512
CHUNK = 64
TB = 512
SCAN_PAIRS = 2
SLAB = 64
FF_CHUNK = 1024
VMEM_LIMIT = 56 * 1024 * 1024


def _cparams(n_axes):
    return pltpu.CompilerParams(dimension_semantics=("arbitrary",) * n_axes,
                                vmem_limit_bytes=VMEM_LIMIT)


def _const_spec(shape):
    nd = len(shape)
    return pl.BlockSpec(shape, lambda *_: (0,) * nd, pipeline_mode=pl.Buffered(1))


def _dot(a, b):
    return jnp.dot(a.astype(BF), b.astype(BF), preferred_element_type=F32)


def _dot_nt(a, b):
    return lax.dot_general(a.astype(BF), b.astype(BF), (((1,), (1,)), ((), ())),
                           preferred_element_type=F32)


def _bmm(a, b):
    return lax.dot_general(a.astype(BF), b.astype(BF), (((2,), (1,)), ((0,), (0,))),
                           preferred_element_type=F32)


def _bmm_nt(a, b):
    return lax.dot_general(a.astype(BF), b.astype(BF), (((2,), (2,)), ((0,), (0,))),
                           preferred_element_type=F32)


def _bmm_tn(a, b):
    return lax.dot_general(a.astype(BF), b.astype(BF), (((1,), (1,)), ((0,), (0,))),
                           preferred_element_type=F32)


def _split2(x):
    hi = x.astype(BF)
    lo = (x - hi.astype(F32)).astype(BF)
    return hi, lo


def _split3(x):
    hi = x.astype(BF)
    r1 = x - hi.astype(F32)
    mid = r1.astype(BF)
    lo = (r1 - mid.astype(F32)).astype(BF)
    return hi, mid, lo


def _dot_exact_rhs(x, sel):
    hi, lo = _split2(x)
    return (jnp.dot(hi, sel, preferred_element_type=F32)
            + jnp.dot(lo, sel, preferred_element_type=F32))


def _dot_exact_lhs(sel, x):
    hi, lo = _split2(x)
    return (jnp.dot(sel, hi, preferred_element_type=F32)
            + jnp.dot(sel, lo, preferred_element_type=F32))


def _rms(x, g):
    return x * lax.rsqrt(jnp.mean(x * x, axis=-1, keepdims=True) + RMS_EPS) * g


def _log_sigmoid(x):
    return jnp.minimum(x, 0.0) - jnp.log(1.0 + jnp.exp(-jnp.abs(x)))


def _sigmoid(x):
    return 1.0 / (1.0 + jnp.exp(-x))


def _fox_in_kernel(y_ref, g_ref, wqkv_ref, wf_ref, wg_ref, bf_ref, qg_ref, kg_ref, e1_ref, e2_ref,
                   q_ref, k_ref, kb_ref, v_ref, vb_ref, lf_ref, sg_ref, *, q_scale):
    h = _rms(y_ref[...], g_ref[...])
    hb = h.astype(BF)
    e1 = e1_ref[...]
    e2 = e2_ref[...]

    def headnorm(z, gain):
        ms = _dot_exact_rhs(z * z, e1) * (1.0 / HEAD_DIM)
        inv = lax.rsqrt(ms + RMS_EPS)
        return z * _dot_exact_rhs(inv, e2) * gain

    q = jnp.dot(hb, wqkv_ref[:, 0:D_MODEL], preferred_element_type=F32)
    q_ref[...] = (headnorm(q, qg_ref[...]) * q_scale).astype(BF)
    k = jnp.dot(hb, wqkv_ref[:, D_MODEL:2 * D_MODEL], preferred_element_type=F32)
    k = headnorm(k, kg_ref[...])
    k_ref[...] = k
    kb_ref[...] = k.astype(BF)
    v = jnp.dot(hb, wqkv_ref[:, 2 * D_MODEL:3 * D_MODEL], preferred_element_type=F32)
    v_ref[...] = v
    vb_ref[...] = v.astype(BF)
    h_hi, h_lo = _split2(h)
    wf_hi = wf_ref[0]
    wf_lo = wf_ref[1]
    f_logit = (jnp.dot(h_hi, wf_hi, preferred_element_type=F32)
               + jnp.dot(h_lo, wf_hi, preferred_element_type=F32)
               + jnp.dot(h_hi, wf_lo, preferred_element_type=F32))
    lane = lax.broadcasted_iota(jnp.int32, f_logit.shape, 1)
    lf_ref[...] = jnp.where(lane < N_HEADS, _log_sigmoid(f_logit + bf_ref[...]), 0.0)
    gate = jnp.dot(hb, wg_ref[...], preferred_element_type=F32)
    sg_ref[...] = _sigmoid(gate)


def _fox_in(y, g, wqkv, wf, wg, b_f, q_g, k_g, e1, e2, tm, q_scale):
    m = y.shape[0]
    row = lambda w: pl.BlockSpec((tm, w), lambda i: (i, 0))
    outs = [jax.ShapeDtypeStruct((m, D_MODEL), BF),
            jax.ShapeDtypeStruct((m, D_MODEL), F32),
            jax.ShapeDtypeStruct((m, D_MODEL), BF),
            jax.ShapeDtypeStruct((m, D_MODEL), F32),
            jax.ShapeDtypeStruct((m, D_MODEL), BF),
            jax.ShapeDtypeStruct((m, LANES), F32),
            jax.ShapeDtypeStruct((m, D_MODEL), F32)]
    return pl.pallas_call(
        functools.partial(_fox_in_kernel, q_scale=q_scale),
        grid=(m // tm,),
        in_specs=[row(D_MODEL), _const_spec((1, D_MODEL)), _const_spec(wqkv.shape),
                  _const_spec(wf.shape), _const_spec(wg.shape), _const_spec((1, LANES)),
                  _const_spec((1, D_MODEL)), _const_spec((1, D_MODEL)),
                  _const_spec(e1.shape), _const_spec(e2.shape)],
        out_specs=[row(D_MODEL)] * 5 + [row(LANES), row(D_MODEL)],
        out_shape=outs,
        compiler_params=_cparams(1),
        name="fox_in",
    )(y, g, wqkv, wf, wg, b_f, q_g, k_g, e1, e2)


def _fox_cum_kernel(lf_ref, tri_ref, pq_ref, pk_ref, cq_ref, ck_ref, qa_ref, ka_ref, carry_ref):
    @pl.when(pl.program_id(1) == 0)
    def _():
        carry_ref[...] = jnp.zeros_like(carry_ref)

    tri = tri_ref[...]
    x = lf_ref[0]
    c = carry_ref[0:1, :]
    for part in _split3(x):
        c = c + jnp.dot(tri, part, preferred_element_type=F32)
    carry_ref[0:1, :] = c[TCUM - 1:TCUM, :]
    qa = cq_ref[...]
    ka = ck_ref[...]
    for j, part in enumerate(_split3(c * LOG2E)):
        qa = qa + jnp.dot(part, pq_ref[j], preferred_element_type=F32)
        ka = ka + jnp.dot(part, pk_ref[j], preferred_element_type=F32)
    qa_ref[0] = qa.astype(BF)
    ka_ref[0] = ka.astype(BF)


def _fox_cum(lf, tri, pq, pk, cq, ck):
    b, t, _ = lf.shape
    blk = lambda w: pl.BlockSpec((1, TCUM, w), lambda i, j: (i, j, 0))
    return pl.pallas_call(
        _fox_cum_kernel,
        grid=(b, t // TCUM),
        in_specs=[blk(LANES), _const_spec(tri.shape), _const_spec(pq.shape), _const_spec(pk.shape),
                  _const_spec(cq.shape), _const_spec(ck.shape)],
        out_specs=[blk(D_MODEL), blk(D_MODEL)],
        out_shape=[jax.ShapeDtypeStruct((b, t, D_MODEL), BF)] * 2,
        scratch_shapes=[pltpu.VMEM((8, LANES), F32)],
        compiler_params=_cparams(2),
        name="fox_cum",
    )(lf, tri, pq, pk, cq, ck)


def _flash_kernel(q_ref, qa_ref, k_ref, ka_ref, v_ref, sg_ref, o_ref,
                  kf_ref, vx_ref, qh_ref, s_ref, m_ref, acc_ref):
    qi = pl.program_id(2)

    @pl.when(qi == 0)
    def _():
        kf_ref[:, 0:LANES] = k_ref[0]
        kf_ref[:, LANES:2 * LANES] = ka_ref[0]
        vx_ref[:, 0:LANES] = v_ref[0]
        vx_ref[:, LANES:2 * LANES] = jnp.ones((vx_ref.shape[0], LANES), BF)

    lane2 = lax.broadcasted_iota(jnp.int32, (1, 2 * LANES), 1)
    first = (lane2 % LANES) < HEAD_DIM
    qf = jnp.concatenate([q_ref[0], qa_ref[0]], axis=1)
    zero = jnp.zeros_like(qf)
    qh_ref[0:TQ, :] = jnp.where(first, qf, zero)
    qh_ref[TQ:2 * TQ, :] = jnp.where(first, zero, qf)
    m_ref[...] = jnp.full(m_ref.shape, NEG, F32)
    acc_ref[...] = jnp.zeros(acc_ref.shape, F32)
    groups = [(0, TQ), (TQ, 2 * TQ)]

    def scores(kb, slot):
        kblk = kf_ref[pl.ds(pl.multiple_of(kb * TQ, TQ), TQ), :]
        for lo, hi in groups:
            s_ref[slot, lo:hi, :] = lax.dot_general(qh_ref[lo:hi, :], kblk, (((1,), (1,)), ((), ())),
                                                    preferred_element_type=F32)

    def softmax_pv(kb, slot, diagonal):
        vblk = vx_ref[pl.ds(pl.multiple_of(kb * TQ, TQ), TQ), :]
        for lo, hi in groups:
            p_parts, al_parts = [], []
            for r0 in range(lo, hi, SLAB):
                s = s_ref[slot, r0:r0 + SLAB, :]
                if diagonal:
                    row = (r0 % TQ) + lax.broadcasted_iota(jnp.int32, (SLAB, TQ), 0)
                    col = lax.broadcasted_iota(jnp.int32, (SLAB, TQ), 1)
                    s = jnp.where(col <= row, s, NEG)
                m_prev = m_ref[r0:r0 + SLAB, :]
                m_next = jnp.maximum(m_prev, jnp.max(s, axis=1, keepdims=True))
                p_parts.append(jnp.exp2(s - jnp.concatenate([m_next] * (TQ // LANES), axis=1)).astype(BF))
                al_parts.append(jnp.exp2(m_prev - m_next))
                m_ref[r0:r0 + SLAB, :] = m_next
            pv = jnp.dot(jnp.concatenate(p_parts, axis=0), vblk, preferred_element_type=F32)
            alpha = jnp.concatenate(al_parts, axis=0)
            acc_ref[lo:hi, :] = acc_ref[lo:hi, :] * jnp.concatenate([alpha, alpha], axis=1) + pv

    def body(kb, carry):
        slot = lax.rem(kb, 2)
        softmax_pv(kb, slot, False)
        scores(kb + 1, 1 - slot)
        return carry

    scores(0, 0)
    lax.fori_loop(0, qi, body, 0)
    softmax_pv(qi, lax.rem(qi, 2), True)
    lane = lax.broadcasted_iota(jnp.int32, (1, LANES), 1)
    o0 = acc_ref[0:TQ, 0:LANES] / acc_ref[0:TQ, LANES:2 * LANES]
    o1 = acc_ref[TQ:2 * TQ, 0:LANES] / acc_ref[TQ:2 * TQ, LANES:2 * LANES]
    o_ref[0] = (jnp.where(lane < HEAD_DIM, o0, o1) * sg_ref[0]).astype(BF)


def _flash(q, qa, kb, ka, vb, sg):
    b, t, _ = q.shape
    qblk = pl.BlockSpec((1, TQ, LANES), lambda i, p, j: (i, j, p))
    kblk = pl.BlockSpec((1, t, LANES), lambda i, p, j: (i, 0, p))
    return pl.pallas_call(
        _flash_kernel,
        grid=(b, N_PAIRS, t // TQ),
        in_specs=[qblk, qblk, kblk, kblk, kblk, qblk],
        out_specs=qblk,
        out_shape=jax.ShapeDtypeStruct((b, t, D_MODEL), BF),
        scratch_shapes=[pltpu.VMEM((t, 2 * LANES), BF),
                        pltpu.VMEM((t, 2 * LANES), BF),
                        pltpu.VMEM((2 * TQ, 2 * LANES), BF),
                        pltpu.VMEM((2, 2 * TQ, TQ), F32),
                        pltpu.VMEM((2 * TQ, LANES), F32),
                        pltpu.VMEM((2 * TQ, 2 * LANES), F32)],
        compiler_params=_cparams(3),
        name="fox_flash",
    )(q, qa, kb, ka, vb, sg)


def _decode_kernel(pt_ref, q_ref, kn_ref, vn_ref, lfn_ref, sg_ref, su_ref, eye_ref, *rest):
    del pt_ref
    n = (len(rest) - 3) // 3
    k_refs, v_refs, lf_refs = rest[0:n], rest[n:2 * n], rest[2 * n:3 * n]
    o_ref, s_ref, p_ref = rest[3 * n:]
    q = q_ref[0]
    eye = eye_ref[...]
    tail = lfn_ref[0]
    for j in reversed(range(n)):
        lf = lf_refs[j][0, 0]
        s_ref[j] = tail + _dot_exact_rhs(lf, su_ref[...])
        tail = tail + jnp.sum(lf, axis=1, keepdims=True)
    for h in range(N_HEADS):
        q_col = jnp.sum(eye * q[h:h + 1, :], axis=1, keepdims=True)
        for j in range(n):
            s_ref[j, h:h + 1, :] += jnp.sum(k_refs[j][0, 0, h] * q_col, axis=0, keepdims=True)
    s = s_ref[...]
    s_new = jnp.sum(q * kn_ref[0], axis=-1, keepdims=True)
    m = jnp.maximum(jnp.max(jnp.max(s, axis=0), axis=1, keepdims=True), s_new)
    p = jnp.exp(s - m[None])
    p_ref[...] = p
    p_new = jnp.exp(s_new - m)
    inv_l = 1.0 / (jnp.sum(jnp.sum(p, axis=0), axis=1, keepdims=True) + p_new)
    new_part = p_new * vn_ref[0]
    scale = inv_l * sg_ref[0]
    for h in range(N_HEADS):
        acc = p_ref[0, h:h + 1, :] * v_refs[0][0, 0, h]
        for j in range(1, n):
            acc = acc + p_ref[j, h:h + 1, :] * v_refs[j][0, 0, h]
        o_col = jnp.sum(acc, axis=1, keepdims=True)
        o_row = jnp.sum(eye * o_col, axis=0, keepdims=True)
        o_ref[0, h:h + 1, :] = (o_row + new_part[h:h + 1, :]) * scale[h:h + 1, :]


def _decode(page_table, layer, q, k_new, v_new, lf_new, sg, cache_kt, cache_vt, logf_t, su, eye):
    n_s, n_pages = page_table.shape
    tok = pl.BlockSpec((1, N_HEADS, HEAD_DIM), lambda i, pt: (i, 0, 0))
    page = lambda j: pl.BlockSpec((1, 1, N_HEADS, HEAD_DIM, PAGE), lambda i, pt: (layer, pt[i, j], 0, 0, 0))
    lfpage = lambda j: pl.BlockSpec((1, 1, N_HEADS, PAGE), lambda i, pt: (layer, pt[i, j], 0, 0))
    pages = list(range(n_pages))
    grid_spec = pltpu.PrefetchScalarGridSpec(
        num_scalar_prefetch=1,
        grid=(n_s,),
        in_specs=[tok, tok, tok, pl.BlockSpec((1, N_HEADS, 1), lambda i, pt: (i, 0, 0)), tok,
                  pl.BlockSpec(su.shape, lambda i, pt: (0, 0)), pl.BlockSpec(eye.shape, lambda i, pt: (0, 0))]
        + [page(j) for j in pages] + [page(j) for j in pages] + [lfpage(j) for j in pages],
        out_specs=tok,
        scratch_shapes=[pltpu.VMEM((n_pages, N_HEADS, PAGE), F32), pltpu.VMEM((n_pages, N_HEADS, PAGE), F32)])
    return pl.pallas_call(
        _decode_kernel,
        grid_spec=grid_spec,
        out_shape=jax.ShapeDtypeStruct((n_s, N_HEADS, HEAD_DIM), F32),
        compiler_params=_cparams(1),
        name="fox_decode",
    )(page_table, q, k_new, v_new, lf_new, sg, su, eye,
      *([cache_kt] * n_pages), *([cache_vt] * n_pages), *([logf_t] * n_pages))


def _out_mlp_kernel(y_ref, a_ref, wo_ref, g_ref, w1_ref, w2_ref, go_ref, o_ref, *, final_norm):
    y = y_ref[...] + jnp.dot(a_ref[...].astype(BF), wo_ref[...], preferred_element_type=F32)
    hb = _rms(y, g_ref[...]).astype(BF)
    acc = y
    for c in range(D_FF // FF_CHUNK):
        u = jnp.dot(hb, w1_ref[:, c * FF_CHUNK:(c + 1) * FF_CHUNK], preferred_element_type=F32)
        u = jnp.maximum(u, 0.0)
        acc = acc + jnp.dot((u * u).astype(BF), w2_ref[c * FF_CHUNK:(c + 1) * FF_CHUNK, :],
                            preferred_element_type=F32)
    if final_norm:
        acc = _rms(acc, go_ref[...])
    o_ref[...] = acc


def _out_mlp(y, a, wo, g, w1, w2, g_out, final_norm, tm):
    m = y.shape[0]
    row = pl.BlockSpec((tm, D_MODEL), lambda i: (i, 0))
    return pl.pallas_call(
        functools.partial(_out_mlp_kernel, final_norm=final_norm),
        grid=(m // tm,),
        in_specs=[row, row, _const_spec(wo.shape), _const_spec((1, D_MODEL)), _const_spec(w1.shape),
                  _const_spec(w2.shape), _const_spec((1, D_MODEL))],
        out_specs=row,
        out_shape=jax.ShapeDtypeStruct((m, D_MODEL), F32),
        compiler_params=_cparams(1),
        name="out_mlp",
    )(y, a, wo, g, w1, w2, g_out)


def _rwkv_in_kernel(*refs, prev_normed, v_gate, tail):
    if v_gate:
        (y_ref, yp_ref, g_ref, mu_ref, wrkv_ref, w0_ref, w1_ref, w2_ref, a0_ref, a1_ref, a2_ref,
         g1_ref, g2_ref, kk_ref, ka_ref, e1_ref, e2_ref, v0_ref, v1_ref, v2_ref, vf_ref,
         r_out, lw_out, k_out, v_out, kap_out, b_out, g_out, h_out) = refs
    else:
        (y_ref, yp_ref, g_ref, mu_ref, wrkv_ref, w0_ref, w1_ref, w2_ref, a0_ref, a1_ref, a2_ref,
         g1_ref, g2_ref, kk_ref, ka_ref, e1_ref, e2_ref,
         r_out, lw_out, k_out, v_out, kap_out, b_out, g_out, h_out) = refs
    h = _rms(y_ref[...], g_ref[...])
    hp = yp_ref[...] if prev_normed else _rms(yp_ref[...], g_ref[...])
    d = hp - h
    tm = h.shape[0]
    h_out[0] = h[tm - tail:tm, :]
    mix = lambda c: h + d * mu_ref[c:c + 1, :]

    r_out[...] = _dot(mix(0), wrkv_ref[0])
    k = _dot(mix(1), wrkv_ref[1])
    xv = mix(2)
    v = _dot(xv, wrkv_ref[2])
    if v_gate:
        vg = _sigmoid(v0_ref[...] + _dot(_dot(xv, v1_ref[...]), v2_ref[...]))
        v = v + (vf_ref[...] - v) * vg
    v_out[...] = v
    w_pre = w0_ref[...] + _dot(jnp.tanh(_dot(mix(3), w1_ref[...])), w2_ref[...])
    lw_out[...] = -jnp.exp(_log_sigmoid(w_pre) - 0.5)
    a = _sigmoid(a0_ref[...] + _dot(_dot(mix(4), a1_ref[...]), a2_ref[...]))
    g_out[...] = _dot(_sigmoid(_dot(mix(5), g1_ref[...])), g2_ref[...])
    kk = k * kk_ref[...]
    ss = _dot_exact_rhs(kk * kk, e1_ref[...])
    inv = 1.0 / jnp.maximum(jnp.sqrt(ss), L2_EPS)
    kap = kk * _dot_exact_rhs(inv, e2_ref[...])
    kap_out[...] = kap
    b_out[...] = kap * a
    k_out[...] = k * (1.0 + (a - 1.0) * ka_ref[...])


def _rwkv_in(y, yprev, g, p, e1, e2, vgate, v_first, prev_normed, tm, tail):
    m = y.shape[0]
    row = pl.BlockSpec((tm, D_MODEL), lambda i: (i, 0))
    vec = _const_spec((1, D_MODEL))
    args = [y, yprev, g, p["mu"], p["w_rkv"], p["w0"], p["w1"], p["w2"], p["a0"], p["a1"], p["a2"],
            p["g1"], p["g2"], p["k_k"], p["k_a"], e1, e2]
    specs = [row, row, vec, _const_spec(p["mu"].shape), _const_spec(p["w_rkv"].shape), vec,
             _const_spec(p["w1"].shape), _const_spec(p["w2"].shape), vec, _const_spec(p["a1"].shape),
             _const_spec(p["a2"].shape), _const_spec(p["g1"].shape), _const_spec(p["g2"].shape),
             vec, vec, _const_spec(e1.shape), _const_spec(e2.shape)]
    if vgate is not None:
        args += [vgate["v0"], vgate["v1"], vgate["v2"], v_first]
        specs += [vec, _const_spec(vgate["v1"].shape), _const_spec(vgate["v2"].shape), row]
    n_tiles = m // tm
    outs = [jax.ShapeDtypeStruct((m, D_MODEL), F32)] * 7 + [jax.ShapeDtypeStruct((n_tiles, tail, D_MODEL), F32)]
    return pl.pallas_call(
        functools.partial(_rwkv_in_kernel, prev_normed=prev_normed, v_gate=vgate is not None, tail=tail),
        grid=(n_tiles,),
        in_specs=specs,
        out_specs=[row] * 7 + [pl.BlockSpec((1, tail, D_MODEL), lambda i: (i, 0, 0))],
        out_shape=outs,
        compiler_params=_cparams(1),
        name="rwkv_in",
    )(*args)


def _unit_lower_inverse(a_b, eye, dmask):
    ad = a_b * dmask
    low = a_b - ad
    a2 = _bmm(ad, ad)
    a4 = _bmm(a2, a2)
    a8 = _bmm(a4, a4)
    x = eye - ad
    x = x + _bmm(x, a2)
    x = x + _bmm(x, a4)
    dinv = x + _bmm(x, a8)
    n = _bmm(dinv, low)
    n2 = _bmm(n, n)
    y = dinv + _bmm(n2, dinv)
    return y - _bmm(n, y)


def _group_norm_gate(y, r, k, v, g, gnw, gnb, rk, eblk):
    mean = _dot_exact_rhs(y, eblk)
    dev = y - mean
    var = _dot_exact_rhs(dev * dev, eblk)
    yn = dev * lax.rsqrt(var + GN_EPS) * gnw + gnb
    bonus = _dot_exact_rhs(r * k * rk, eblk) * float(HEAD_DIM) * v
    return (yn + bonus) * g


def _rwkv_scan_kernel(r_ref, lw_ref, k_ref, v_ref, kap_ref, b_ref, g_ref, gnw_ref, gnb_ref, rk_ref,
                      tril_ref, amask_ref, dmask_ref, eye_ref, eblk_ref, a_out, st_out, st_ref, y_ref):
    tb = pl.program_id(2)

    @pl.when(tb == 0)
    def _():
        st_ref[...] = jnp.zeros_like(st_ref)

    lane = lax.broadcasted_iota(jnp.int32, (1, 1, LANES), 2)
    m0 = (lane < HEAD_DIM).astype(F32)
    m1 = 1.0 - m0
    stack = lambda x: jnp.concatenate([x * m0, x * m1], axis=1)
    nc = TB // CHUNK
    c = CHUNK
    amask = amask_ref[...]
    ld = lambda ref: jnp.concatenate([ref[0][:, :, q * LANES:(q + 1) * LANES] for q in range(SCAN_PAIRS)], axis=0)
    lw = ld(lw_ref)
    r = ld(r_ref)
    k = ld(k_ref)
    v = ld(v_ref)
    kap = ld(kap_ref)
    bb = ld(b_ref)
    tril = jnp.broadcast_to(tril_ref[...], (SCAN_PAIRS * nc, c, c))
    lw_hi, lw_lo = _split2(lw)
    cl = _bmm(tril, lw_hi) + _bmm(tril, lw_lo)
    cmid = cl[:, c // 2 - 1:c // 2, :]
    clast = cl[:, c - 1:c, :]
    e_r = jnp.exp(cl - cmid)
    e_k = jnp.exp(cmid - cl)
    e_ka = jnp.exp(cl - lw - cmid)
    dc0 = jnp.exp(cmid)
    dc1 = jnp.exp(clast - cmid)
    gam = jnp.exp(clast)
    r_t = r * e_r
    ka_t = kap * e_ka
    k_t = k * e_k
    b_t = bb * e_k
    r_true = stack(r_t * dc0)
    ka_true = stack(ka_t * dc0)
    k_hat = stack(k_t * dc1)
    b_hat = stack(b_t * dc1)
    v_bd = stack(v)
    pmat = jnp.concatenate([stack(ka_t), stack(r_t)], axis=1)
    qmat = jnp.concatenate([stack(k_t), stack(b_t)], axis=1)
    amat = _bmm_nt(pmat, qmat) * amask
    a_k = amat[:, 0:2 * c, 0:2 * c]
    a_b = amat[:, 0:2 * c, 2 * c:4 * c]
    a_rk = amat[:, 2 * c:4 * c, 0:2 * c]
    a_rb = amat[:, 2 * c:4 * c, 2 * c:4 * c]
    tinv = _unit_lower_inverse(a_b, eye_ref[...], dmask_ref[...])
    akv = _bmm(a_k, v_bd)
    wu = _bmm(tinv, jnp.concatenate([ka_true, akv], axis=2))
    w = wu[:, :, 0:2 * c]
    u0 = wu[:, :, 2 * c:4 * c]
    qy = _bmm(a_rb, wu)
    qe = r_true - qy[:, :, 0:2 * c]
    y0 = _bmm(a_rk, v_bd) - qy[:, :, 2 * c:4 * c]
    wtb = _bmm_tn(w, b_hat)
    hm = _bmm_tn(jnp.concatenate([v_bd, u0], axis=1), jnp.concatenate([k_hat, -b_hat], axis=1))

    sts = [st_ref[q] for q in range(SCAN_PAIRS)]
    for ci in range(nc):
        for q in range(SCAN_PAIRS):
            n = q * nc + ci
            y_bd = _dot_nt(qe[n], sts[q]) + y0[n]
            y_ref[n] = y_bd[0:c, :] + y_bd[c:2 * c, :]
            sts[q] = sts[q] * gam[n] - _dot(sts[q], wtb[n]) + hm[n]
    for q in range(SCAN_PAIRS):
        st_ref[q] = sts[q]

    flat = lambda x: x.reshape(TB, LANES)
    for q in range(SCAN_PAIRS):
        rows = slice(q * nc, (q + 1) * nc)
        lanes = slice(q * LANES, (q + 1) * LANES)
        a = _group_norm_gate(flat(y_ref[rows]), flat(r[rows]), flat(k[rows]), flat(v[rows]),
                             flat(g_ref[0, :, :, lanes]), gnw_ref[:, lanes], gnb_ref[:, lanes],
                             rk_ref[:, lanes], eblk_ref[...])
        a_out[0, :, :, lanes] = a.reshape(nc, c, LANES).astype(BF)

    @pl.when(tb == pl.num_programs(2) - 1)
    def _():
        for q in range(SCAN_PAIRS):
            st_out[0, q] = sts[q]


def _rwkv_scan(r, lw, k, v, kap, bb, g, gnw, gnb, rk, consts):
    b, n_chunks = r.shape[0], r.shape[1]
    nc = TB // CHUNK
    blk = pl.BlockSpec((1, nc, CHUNK, SCAN_PAIRS * LANES), lambda i, p, j: (i, j, 0, p))
    vec = pl.BlockSpec((1, SCAN_PAIRS * LANES), lambda i, p, j: (0, p))
    tril, amask, dmask, eye, eblk = consts
    return pl.pallas_call(
        _rwkv_scan_kernel,
        grid=(b, N_PAIRS // SCAN_PAIRS, n_chunks // nc),
        in_specs=[blk] * 7 + [vec] * 3 + [_const_spec(x.shape) for x in consts],
        out_specs=[blk, pl.BlockSpec((1, SCAN_PAIRS, LANES, LANES), lambda i, p, j: (i, p, 0, 0))],
        out_shape=[jax.ShapeDtypeStruct(r.shape, BF),
                   jax.ShapeDtypeStruct((b, N_PAIRS, LANES, LANES), F32)],
        scratch_shapes=[pltpu.VMEM((SCAN_PAIRS, LANES, LANES), F32),
                        pltpu.VMEM((SCAN_PAIRS * nc, CHUNK, LANES), F32)],
        compiler_params=_cparams(3),
        name="rwkv_scan",
    )(r, lw, k, v, kap, bb, g, gnw, gnb, rk, tril, amask, dmask, eye, eblk)


def _rwkv_step_kernel(s_ref, r_ref, lw_ref, k_ref, v_ref, kap_ref, b_ref, g_ref, gnw_ref, gnb_ref,
                      rk_ref, eye_ref, a_out, s_out):
    s = s_ref[0]
    eye = eye_ref[...][None]
    bc = lambda x_ref: x_ref[0][:, None, :]
    s_kk = -jnp.sum(s * bc(kap_ref), axis=-1, keepdims=True)
    v_col = jnp.sum(eye * bc(v_ref), axis=-1, keepdims=True)
    s_new = s * jnp.exp(bc(lw_ref)) + s_kk * bc(b_ref) + v_col * bc(k_ref)
    s_out[0] = s_new
    y_col = jnp.sum(s_new * bc(r_ref), axis=-1, keepdims=True)
    y = jnp.sum(eye * y_col, axis=1)
    r = r_ref[0]
    k = k_ref[0]
    v = v_ref[0]
    mean = jnp.mean(y, axis=-1, keepdims=True)
    dev = y - mean
    var = jnp.mean(dev * dev, axis=-1, keepdims=True)
    yn = dev * lax.rsqrt(var + GN_EPS) * gnw_ref[...] + gnb_ref[...]
    bonus = jnp.sum(r * k * rk_ref[...], axis=-1, keepdims=True) * v
    a_out[0] = (yn + bonus) * g_ref[0]


def _rwkv_step(state, r, lw, k, v, kap, bb, g, gnw, gnb, rk, eye):
    n_s = state.shape[0]
    tok = pl.BlockSpec((1, N_HEADS, HEAD_DIM), lambda i: (i, 0, 0))
    st = pl.BlockSpec((1, N_HEADS, HEAD_DIM, HEAD_DIM), lambda i: (i, 0, 0, 0))
    par = pl.BlockSpec((N_HEADS, HEAD_DIM), lambda i: (0, 0))
    return pl.pallas_call(
        _rwkv_step_kernel,
        grid=(n_s,),
        in_specs=[st] + [tok] * 7 + [par] * 3 + [pl.BlockSpec((HEAD_DIM, HEAD_DIM), lambda i: (0, 0))],
        out_specs=[tok, st],
        out_shape=[jax.ShapeDtypeStruct((n_s, N_HEADS, HEAD_DIM), F32),
                   jax.ShapeDtypeStruct(state.shape, F32)],
        compiler_params=_cparams(1),
        name="rwkv_step",
    )(state, r, lw, k, v, kap, bb, g, gnw, gnb, rk, eye)


def _head_selectors():
    d = jnp.arange(D_MODEL)
    e1 = (d[:, None] // HEAD_DIM == jnp.arange(LANES)[None, :]).astype(BF)
    return e1, e1.T


def _aug_placement():
    h = jnp.arange(LANES)
    col = jnp.arange(D_MODEL)
    base = (h // 2) * LANES + (h % 2) * HEAD_DIM
    valid = (h < N_HEADS)[:, None]
    place = lambda off: (valid & (col[None, :] == (base + off)[:, None])).astype(BF)
    pq = jnp.stack([place(0), place(1), place(2)])
    pk = jnp.stack([place(3), place(4), place(5)])
    within = col % HEAD_DIM
    cq = jnp.where((within >= 3) & (within < 6), -1.0, 0.0).astype(F32)[None, :]
    ck = jnp.where(within < 3, 1.0, 0.0).astype(F32)[None, :]
    return pq, pk, cq, ck


def _scan_constants():
    c = CHUNK
    i = jnp.arange(c)
    tril = (i[None, :] <= i[:, None]).astype(BF)
    r4 = jnp.arange(4 * c)
    same = (r4[:, None] // c) % 2 == (r4[None, :] // c) % 2
    tr = r4[:, None] % c
    ti = r4[None, :] % c
    strict = tr > ti
    incl = tr >= ti
    amask = (same & jnp.where((r4 < 2 * c)[:, None], strict, incl)).astype(F32)
    r2 = jnp.arange(2 * c)
    dmask = (r2[:, None] // 16 == r2[None, :] // 16).astype(F32)
    eye = jnp.eye(2 * c, dtype=F32)
    eblk = ((r2[:, None] // HEAD_DIM == r2[None, :] // HEAD_DIM).astype(F32) / HEAD_DIM).astype(BF)
    return tril, amask, dmask, eye, eblk


def _pad_cols(w, n):
    return jnp.pad(w, ((0, 0), (0, n - w.shape[1])))


def _pad_rows(w, n):
    return jnp.pad(w, ((0, n - w.shape[0]), (0, 0)))


def _heads(x):
    return x.reshape(x.shape[0], N_HEADS, HEAD_DIM)


def kernel(x_prompt, x_sample, cache_k, cache_v, cache_logf, page_table, state_shift, state_wkv,
           norm_mix, norm_mlp, norm_out, fox_w_in, fox_b_f, fox_q_norm, fox_k_norm, fox_w_o,
           rwkv_mu, rwkv_w_rkv, rwkv_w0, rwkv_w1, rwkv_w2, rwkv_a0, rwkv_a1, rwkv_a2,
           rwkv_v0, rwkv_v1, rwkv_v2, rwkv_g1, rwkv_g2, rwkv_k_k, rwkv_k_a, rwkv_r_k,
           rwkv_gn_w, rwkv_gn_b, rwkv_w_o, mlp_w1, mlp_w2):
    b, t, _ = x_prompt.shape
    n_s = x_sample.shape[0]
    depth = norm_mix.shape[0]
    yp = x_prompt.reshape(b * t, D_MODEL)
    ys = x_sample.reshape(n_s, D_MODEL)
    vec = lambda x: x.reshape(1, -1).astype(F32)
    e1, e2 = _head_selectors()
    pq, pk, cq, ck = _aug_placement()
    scan_consts = _scan_constants()
    tri_cum = (jnp.arange(TCUM)[None, :] <= jnp.arange(TCUM)[:, None]).astype(BF)
    tok = jnp.arange(PAGE)
    suffix = (tok[:, None] > tok[None, :]).astype(BF)
    eye64 = jnp.eye(HEAD_DIM, dtype=F32)
    logf_t = jnp.swapaxes(cache_logf, 2, 3)
    cache_kt = jnp.transpose(cache_k, (0, 1, 3, 4, 2))
    cache_vt = jnp.transpose(cache_v, (0, 1, 3, 4, 2))

    k_p, v_p, lf_p, k_s, v_s, lf_s = [], [], [], [], [], []
    sh_p, wkv_p, sh_s, wkv_s = [], [], [], []
    vf_p = vf_s = None
    for layer in range(depth):
        i = layer // 2
        g_mix = vec(norm_mix[layer])
        if layer % 2 == 0:
            w_in = fox_w_in[i]
            wqkv = w_in[:, :3 * D_MODEL].astype(BF)
            wf_full = _pad_cols(w_in[:, 3 * D_MODEL:3 * D_MODEL + N_HEADS], LANES)
            wf = jnp.stack(_split2(wf_full))
            wg = w_in[:, 3 * D_MODEL + N_HEADS:].astype(BF)
            b_f = _pad_cols(vec(fox_b_f[i]), LANES)
            q_g = vec(jnp.tile(fox_q_norm[i], N_HEADS))
            k_g = vec(jnp.tile(fox_k_norm[i], N_HEADS))
            fox_args = (g_mix, wqkv, wf, wg, b_f, q_g, k_g, e1, e2)
            q, k, kb, v, vb, lf, sg = _fox_in(yp, *fox_args, tm=TM, q_scale=ATTN_SCALE * LOG2E)
            qa, ka = _fox_cum(lf.reshape(b, t, LANES), tri_cum, pq, pk, cq, ck)
            shp = lambda x: x.reshape(b, t, D_MODEL)
            mix_p = _flash(shp(q), qa, shp(kb), ka, shp(vb), shp(sg)).reshape(b * t, D_MODEL)
            k_p.append(k.reshape(b, t, N_HEADS, HEAD_DIM))
            v_p.append(v.reshape(b, t, N_HEADS, HEAD_DIM))
            lf_p.append(lf[:, :N_HEADS].reshape(b, t, N_HEADS))
            q, k, kb, v, vb, lf, sg = _fox_in(ys, *fox_args, tm=n_s, q_scale=ATTN_SCALE)
            lf16 = lf[:, :N_HEADS]
            mix_s = _decode(page_table, i, _heads(q.astype(F32)), _heads(k), _heads(v),
                            lf16[:, :, None], _heads(sg), cache_kt, cache_vt, logf_t, suffix, eye64)
            mix_s = mix_s.reshape(n_s, D_MODEL)
            k_s.append(k.reshape(n_s, 1, N_HEADS, HEAD_DIM))
            v_s.append(v.reshape(n_s, 1, N_HEADS, HEAD_DIM))
            lf_s.append(lf16.reshape(n_s, 1, N_HEADS))
            w_o = fox_w_o[i].astype(BF)
        else:
            p = {
                "mu": _pad_rows(rwkv_mu[i], 8),
                "w_rkv": rwkv_w_rkv[i].astype(BF),
                "w0": vec(rwkv_w0[i]), "a0": vec(rwkv_a0[i]),
                "w1": _pad_cols(rwkv_w1[i], LANES).astype(BF), "w2": _pad_rows(rwkv_w2[i], LANES).astype(BF),
                "a1": _pad_cols(rwkv_a1[i], LANES).astype(BF), "a2": _pad_rows(rwkv_a2[i], LANES).astype(BF),
                "g1": _pad_cols(rwkv_g1[i], 2 * LANES).astype(BF), "g2": _pad_rows(rwkv_g2[i], 2 * LANES).astype(BF),
                "k_k": vec(rwkv_k_k[i]), "k_a": vec(rwkv_k_a[i]),
            }
            vgate = None
            if i > 0:
                vgate = {"v0": vec(rwkv_v0[i - 1]),
                         "v1": _pad_cols(rwkv_v1[i - 1], LANES).astype(BF),
                         "v2": _pad_rows(rwkv_v2[i - 1], LANES).astype(BF)}
            gnw, gnb, rk = vec(rwkv_gn_w[i]), vec(rwkv_gn_b[i]), vec(rwkv_r_k[i])
            yp3 = yp.reshape(b, t, D_MODEL)
            yprev = jnp.concatenate([jnp.zeros_like(yp3[:, :1]), yp3[:, :-1]], axis=1).reshape(b * t, D_MODEL)
            r, lw, k, v, kap, bb, g, htail = _rwkv_in(yp, yprev, g_mix, p, e1, e2, vgate, vf_p,
                                                      prev_normed=False, tm=TM, tail=8)
            if i == 0:
                vf_p = v
            shp = lambda x: x.reshape(b, t // CHUNK, CHUNK, D_MODEL)
            mix_p, st = _rwkv_scan(shp(r), shp(lw), shp(k), shp(v), shp(kap), shp(bb), shp(g),
                                   gnw, gnb, rk, scan_consts)
            mix_p = mix_p.reshape(b * t, D_MODEL)
            sh_p.append(htail.reshape(b, t // TM, 8, D_MODEL)[:, -1, -1, :])
            st = st.reshape(b, N_PAIRS, 2, HEAD_DIM, 2, HEAD_DIM)
            wkv_p.append(jnp.stack([st[:, :, 0, :, 0, :], st[:, :, 1, :, 1, :]], axis=2)
                         .reshape(b, N_HEADS, HEAD_DIM, HEAD_DIM))
            r, lw, k, v, kap, bb, g, htail = _rwkv_in(ys, state_shift[i], g_mix, p, e1, e2, vgate, vf_s,
                                                      prev_normed=True, tm=n_s, tail=n_s)
            if i == 0:
                vf_s = v
            hd = lambda x: x.reshape(N_HEADS, HEAD_DIM)
            mix_s, st_s = _rwkv_step(state_wkv[i], _heads(r), _heads(lw), _heads(k), _heads(v), _heads(kap),
                                     _heads(bb), _heads(g), hd(gnw), hd(gnb), hd(rk), eye64)
            mix_s = mix_s.reshape(n_s, D_MODEL)
            sh_s.append(htail.reshape(n_s, D_MODEL))
            wkv_s.append(st_s)
            w_o = rwkv_w_o[i].astype(BF)
        last = layer == depth - 1
        mlp_args = (w_o, vec(norm_mlp[layer]), mlp_w1[layer].astype(BF), mlp_w2[layer].astype(BF),
                    vec(norm_out), last)
        yp = _out_mlp(yp, mix_p, *mlp_args, tm=TM)
        ys = _out_mlp(ys, mix_s, *mlp_args, tm=n_s)
    return (yp.reshape(b, t, D_MODEL), ys.reshape(n_s, 1, D_MODEL),
            jnp.stack(k_p), jnp.stack(v_p), jnp.stack(lf_p),
            jnp.stack(k_s), jnp.stack(v_s), jnp.stack(lf_s),
            jnp.stack(sh_p), jnp.stack(wkv_p), jnp.stack(sh_s), jnp.stack(wkv_s))
```

```python
import functools
import math

import jax
import jax.numpy as jnp
from jax import lax
from jax.experimental import pallas as pl
from jax.experimental.pallas import tpu as pltpu

F32 = jnp.float32
BF = jnp.bfloat16

D_MODEL = 1024
HEAD_DIM = 64
N_HEADS = D_MODEL // HEAD_DIM
N_PAIRS = N_HEADS // 2
LANES = 128
D_FF = 4 * D_MODEL
PAGE = 128
ATTN_SCALE = HEAD_DIM ** -0.5
LOG2E = math.log2(math.e)
RMS_EPS = 1e-6
GN_EPS = 64e-5
L2_EPS = 1e-12
NEG = -1e30

TM = 256
TQ = 512
TCUM = 512
CHUNK = 64
TB = 512
SCAN_PAIRS = 2
SLAB = 64
FF_CHUNK = 1024
VMEM_LIMIT = 56 * 1024 * 1024


def _cparams(n_axes):
    return pltpu.CompilerParams(dimension_semantics=("arbitrary",) * n_axes,
                                vmem_limit_bytes=VMEM_LIMIT)


def _const_spec(shape):
    nd = len(shape)
    return pl.BlockSpec(shape, lambda *_: (0,) * nd, pipeline_mode=pl.Buffered(1))


def _dot(a, b):
    return jnp.dot(a.astype(BF), b.astype(BF), preferred_element_type=F32)


def _dot_nt(a, b):
    return lax.dot_general(a.astype(BF), b.astype(BF), (((1,), (1,)), ((), ())),
                           preferred_element_type=F32)


def _bmm(a, b):
    return lax.dot_general(a.astype(BF), b.astype(BF), (((2,), (1,)), ((0,), (0,))),
                           preferred_element_type=F32)


def _bmm_nt(a, b):
    return lax.dot_general(a.astype(BF), b.astype(BF), (((2,), (2,)), ((0,), (0,))),
                           preferred_element_type=F32)


def _bmm_tn(a, b):
    return lax.dot_general(a.astype(BF), b.astype(BF), (((1,), (1,)), ((0,), (0,))),
                           preferred_element_type=F32)


def _split2(x):
    hi = x.astype(BF)
    lo = (x - hi.astype(F32)).astype(BF)
    return hi, lo


def _split3(x):
    hi = x.astype(BF)
    r1 = x - hi.astype(F32)
    mid = r1.astype(BF)
    lo = (r1 - mid.astype(F32)).astype(BF)
    return hi, mid, lo


def _dot_exact_rhs(x, sel):
    hi, lo = _split2(x)
    return (jnp.dot(hi, sel, preferred_element_type=F32)
            + jnp.dot(lo, sel, preferred_element_type=F32))


def _dot_exact_lhs(sel, x):
    hi, lo = _split2(x)
    return (jnp.dot(sel, hi, preferred_element_type=F32)
            + jnp.dot(sel, lo, preferred_element_type=F32))


def _rms(x, g):
    return x * lax.rsqrt(jnp.mean(x * x, axis=-1, keepdims=True) + RMS_EPS) * g


def _log_sigmoid(x):
    return jnp.minimum(x, 0.0) - jnp.log(1.0 + jnp.exp(-jnp.abs(x)))


def _sigmoid(x):
    return 1.0 / (1.0 + jnp.exp(-x))


def _fox_in_kernel(*refs, q_scale, aliased, tokens_out):
    y_ref, g_ref, wqkv_ref, wf_ref, wg_ref, bf_ref, qg_ref, kg_ref, e1_ref, e2_ref = refs[:10]
    outs = refs[12:] if aliased else refs[10:]
    q_ref, kb_ref, vb_ref, lf_ref, sg_ref, kt_ref, vt_ref = outs[:7]
    h = _rms(y_ref[...], g_ref[...])
    hb = h.astype(BF)
    e1 = e1_ref[...]
    e2 = e2_ref[...]

    def headnorm(z, gain):
        ms = _dot_exact_rhs(z * z, e1) * (1.0 / HEAD_DIM)
        inv = lax.rsqrt(ms + RMS_EPS)
        return z * _dot_exact_rhs(inv, e2) * gain

    q = jnp.dot(hb, wqkv_ref[:, 0:D_MODEL], preferred_element_type=F32)
    q_ref[...] = (headnorm(q, qg_ref[...]) * q_scale).astype(BF)
    k = jnp.dot(hb, wqkv_ref[:, D_MODEL:2 * D_MODEL], preferred_element_type=F32)
    k = headnorm(k, kg_ref[...])
    kb_ref[...] = k.astype(BF)
    kt_ref[0, 0] = k.T
    v = jnp.dot(hb, wqkv_ref[:, 2 * D_MODEL:3 * D_MODEL], preferred_element_type=F32)
    vb_ref[...] = v.astype(BF)
    vt_ref[0, 0] = v.T
    if tokens_out:
        outs[7][...] = k
        outs[8][...] = v
    h_hi, h_lo = _split2(h)
    wf_hi = wf_ref[0]
    wf_lo = wf_ref[1]
    f_logit = (jnp.dot(h_hi, wf_hi, preferred_element_type=F32)
               + jnp.dot(h_lo, wf_hi, preferred_element_type=F32)
               + jnp.dot(h_hi, wf_lo, preferred_element_type=F32))
    lane = lax.broadcasted_iota(jnp.int32, f_logit.shape, 1)
    lf_ref[...] = jnp.where(lane < N_HEADS, _log_sigmoid(f_logit + bf_ref[...]), 0.0)
    gate = jnp.dot(hb, wg_ref[...], preferred_element_type=F32)
    sg_ref[...] = _sigmoid(gate)


def _fox_in(y, g, wqkv, wf, wg, b_f, q_g, k_g, e1, e2, *, tm, q_scale, seq, slot, n_slots, kt_vt, tokens_out):
    m = y.shape[0]
    tiles = seq // tm
    row = lambda w: pl.BlockSpec((tm, w), lambda i: (i, 0))
    tposed = pl.BlockSpec((1, 1, D_MODEL, tm), lambda i: (slot, i // tiles, 0, i % tiles))
    tshape = jax.ShapeDtypeStruct((n_slots, m // seq, D_MODEL, seq), F32)
    outs = [jax.ShapeDtypeStruct((m, D_MODEL), BF),
            jax.ShapeDtypeStruct((m, D_MODEL), BF),
            jax.ShapeDtypeStruct((m, D_MODEL), BF),
            jax.ShapeDtypeStruct((m, LANES), F32),
            jax.ShapeDtypeStruct((m, D_MODEL), F32),
            tshape, tshape]
    out_specs = [row(D_MODEL)] * 3 + [row(LANES), row(D_MODEL), tposed, tposed]
    if tokens_out:
        outs += [jax.ShapeDtypeStruct((m, D_MODEL), F32)] * 2
        out_specs += [row(D_MODEL)] * 2
    args = [y, g, wqkv, wf, wg, b_f, q_g, k_g, e1, e2]
    in_specs = [row(D_MODEL), _const_spec((1, D_MODEL)), _const_spec(wqkv.shape),
                _const_spec(wf.shape), _const_spec(wg.shape), _const_spec((1, LANES)),
                _const_spec((1, D_MODEL)), _const_spec((1, D_MODEL)),
                _const_spec(e1.shape), _const_spec(e2.shape)]
    aliases = {}
    if kt_vt is not None:
        args += list(kt_vt)
        in_specs += [pl.BlockSpec(memory_space=pl.ANY)] * 2
        aliases = {10: 5, 11: 6}
    return pl.pallas_call(
        functools.partial(_fox_in_kernel, q_scale=q_scale, aliased=kt_vt is not None, tokens_out=tokens_out),
        grid=(m // tm,),
        in_specs=in_specs,
        out_specs=out_specs,
        out_shape=outs,
        input_output_aliases=aliases,
        compiler_params=_cparams(1),
        name="fox_in",
    )(*args)


def _fox_cum_kernel(lf_ref, tri_ref, pq_ref, pk_ref, cq_ref, ck_ref, qa_ref, ka_ref, carry_ref):
    @pl.when(pl.program_id(1) == 0)
    def _():
        carry_ref[...] = jnp.zeros_like(carry_ref)

    tri = tri_ref[...]
    x = lf_ref[0]
    c = carry_ref[0:1, :]
    for part in _split3(x):
        c = c + jnp.dot(tri, part, preferred_element_type=F32)
    carry_ref[0:1, :] = c[TCUM - 1:TCUM, :]
    qa = cq_ref[...]
    ka = ck_ref[...]
    for j, part in enumerate(_split3(c * LOG2E)):
        qa = qa + jnp.dot(part, pq_ref[j], preferred_element_type=F32)
        ka = ka + jnp.dot(part, pk_ref[j], preferred_element_type=F32)
    qa_ref[0] = qa.astype(BF)
    ka_ref[0] = ka.astype(BF)


def _fox_cum(lf, tri, pq, pk, cq, ck):
    b, t, _ = lf.shape
    blk = lambda w: pl.BlockSpec((1, TCUM, w), lambda i, j: (i, j, 0))
    return pl.pallas_call(
        _fox_cum_kernel,
        grid=(b, t // TCUM),
        in_specs=[blk(LANES), _const_spec(tri.shape), _const_spec(pq.shape), _const_spec(pk.shape),
                  _const_spec(cq.shape), _const_spec(ck.shape)],
        out_specs=[blk(D_MODEL), blk(D_MODEL)],
        out_shape=[jax.ShapeDtypeStruct((b, t, D_MODEL), BF)] * 2,
        scratch_shapes=[pltpu.VMEM((8, LANES), F32)],
        compiler_params=_cparams(2),
        name="fox_cum",
    )(lf, tri, pq, pk, cq, ck)


def _flash_kernel(q_ref, qa_ref, k_ref, ka_ref, v_ref, sg_ref, o_ref,
                  kf_ref, vx_ref, qh_ref, s_ref, m_ref, acc_ref):
    qi = pl.program_id(2)

    @pl.when(qi == 0)
    def _():
        kf_ref[:, 0:LANES] = k_ref[0]
        kf_ref[:, LANES:2 * LANES] = ka_ref[0]
        vx_ref[:, 0:LANES] = v_ref[0]
        vx_ref[:, LANES:2 * LANES] = jnp.ones((vx_ref.shape[0], LANES), BF)

    lane2 = lax.broadcasted_iota(jnp.int32, (1, 2 * LANES), 1)
    first = (lane2 % LANES) < HEAD_DIM
    qf = jnp.concatenate([q_ref[0], qa_ref[0]], axis=1)
    zero = jnp.zeros_like(qf)
    qh_ref[0:TQ, :] = jnp.where(first, qf, zero)
    qh_ref[TQ:2 * TQ, :] = jnp.where(first, zero, qf)
    m_ref[...] = jnp.full(m_ref.shape, NEG, F32)
    acc_ref[...] = jnp.zeros(acc_ref.shape, F32)
    groups = [(0, TQ), (TQ, 2 * TQ)]

    def scores(kb, slot):
        kblk = kf_ref[pl.ds(pl.multiple_of(kb * TQ, TQ), TQ), :]
        for lo, hi in groups:
            s_ref[slot, lo:hi, :] = lax.dot_general(qh_ref[lo:hi, :], kblk, (((1,), (1,)), ((), ())),
                                                    preferred_element_type=F32)

    def softmax_pv(kb, slot, diagonal):
        vblk = vx_ref[pl.ds(pl.multiple_of(kb * TQ, TQ), TQ), :]
        for lo, hi in groups:
            p_parts, al_parts = [], []
            for r0 in range(lo, hi, SLAB):
                s = s_ref[slot, r0:r0 + SLAB, :]
                if diagonal:
                    row = (r0 % TQ) + lax.broadcasted_iota(jnp.int32, (SLAB, TQ), 0)
                    col = lax.broadcasted_iota(jnp.int32, (SLAB, TQ), 1)
                    s = jnp.where(col <= row, s, NEG)
                m_prev = m_ref[r0:r0 + SLAB, :]
                m_next = jnp.maximum(m_prev, jnp.max(s, axis=1, keepdims=True))
                p_parts.append(jnp.exp2(s - jnp.concatenate([m_next] * (TQ // LANES), axis=1)).astype(BF))
                al_parts.append(jnp.exp2(m_prev - m_next))
                m_ref[r0:r0 + SLAB, :] = m_next
            pv = jnp.dot(jnp.concatenate(p_parts, axis=0), vblk, preferred_element_type=F32)
            alpha = jnp.concatenate(al_parts, axis=0)
            acc_ref[lo:hi, :] = acc_ref[lo:hi, :] * jnp.concatenate([alpha, alpha], axis=1) + pv

    def body(kb, carry):
        slot = lax.rem(kb, 2)
        softmax_pv(kb, slot, False)
        scores(kb + 1, 1 - slot)
        return carry

    scores(0, 0)
    lax.fori_loop(0, qi, body, 0)
    softmax_pv(qi, lax.rem(qi, 2), True)
    lane = lax.broadcasted_iota(jnp.int32, (1, LANES), 1)
    o0 = acc_ref[0:TQ, 0:LANES] / acc_ref[0:TQ, LANES:2 * LANES]
    o1 = acc_ref[TQ:2 * TQ, 0:LANES] / acc_ref[TQ:2 * TQ, LANES:2 * LANES]
    o_ref[0] = (jnp.where(lane < HEAD_DIM, o0, o1) * sg_ref[0]).astype(BF)


def _flash(q, qa, kb, ka, vb, sg):
    b, t, _ = q.shape
    qblk = pl.BlockSpec((1, TQ, LANES), lambda i, p, j: (i, j, p))
    kblk = pl.BlockSpec((1, t, LANES), lambda i, p, j: (i, 0, p))
    return pl.pallas_call(
        _flash_kernel,
        grid=(b, N_PAIRS, t // TQ),
        in_specs=[qblk, qblk, kblk, kblk, kblk, qblk],
        out_specs=qblk,
        out_shape=jax.ShapeDtypeStruct((b, t, D_MODEL), BF),
        scratch_shapes=[pltpu.VMEM((t, 2 * LANES), BF),
                        pltpu.VMEM((t, 2 * LANES), BF),
                        pltpu.VMEM((2 * TQ, 2 * LANES), BF),
                        pltpu.VMEM((2, 2 * TQ, TQ), F32),
                        pltpu.VMEM((2 * TQ, LANES), F32),
                        pltpu.VMEM((2 * TQ, 2 * LANES), F32)],
        compiler_params=_cparams(3),
        name="fox_flash",
    )(q, qa, kb, ka, vb, sg)


def _decode_kernel(pt_ref, q_ref, kn_ref, vn_ref, lfn_ref, sg_ref, su_ref, eye_ref, *rest):
    del pt_ref
    n = (len(rest) - 3) // 3
    k_refs, v_refs, lf_refs = rest[0:n], rest[n:2 * n], rest[2 * n:3 * n]
    o_ref, s_ref, p_ref = rest[3 * n:]
    q = q_ref[0]
    eye = eye_ref[...]
    tail = lfn_ref[0]
    for j in reversed(range(n)):
        lf = lf_refs[j][0, 0]
        s_ref[j] = tail + _dot_exact_rhs(lf, su_ref[...])
        tail = tail + jnp.sum(lf, axis=1, keepdims=True)
    for h in range(N_HEADS):
        q_col = jnp.sum(eye * q[h:h + 1, :], axis=1, keepdims=True)
        for j in range(n):
            s_ref[j, h:h + 1, :] += jnp.sum(k_refs[j][0, 0, h] * q_col, axis=0, keepdims=True)
    s = s_ref[...]
    s_new = jnp.sum(q * kn_ref[0], axis=-1, keepdims=True)
    m = jnp.maximum(jnp.max(jnp.max(s, axis=0), axis=1, keepdims=True), s_new)
    p = jnp.exp(s - m[None])
    p_ref[...] = p
    p_new = jnp.exp(s_new - m)
    inv_l = 1.0 / (jnp.sum(jnp.sum(p, axis=0), axis=1, keepdims=True) + p_new)
    new_part = p_new * vn_ref[0]
    scale = inv_l * sg_ref[0]
    for h in range(N_HEADS):
        acc = p_ref[0, h:h + 1, :] * v_refs[0][0, 0, h]
        for j in range(1, n):
            acc = acc + p_ref[j, h:h + 1, :] * v_refs[j][0, 0, h]
        o_col = jnp.sum(acc, axis=1, keepdims=True)
        o_row = jnp.sum(eye * o_col, axis=0, keepdims=True)
        o_ref[0, h:h + 1, :] = (o_row + new_part[h:h + 1, :]) * scale[h:h + 1, :]


def _decode(page_table, layer, q, k_new, v_new, lf_new, sg, cache_kt, cache_vt, logf_t, su, eye):
    n_s, n_pages = page_table.shape
    tok = pl.BlockSpec((1, N_HEADS, HEAD_DIM), lambda i, pt: (i, 0, 0))
    page = lambda j: pl.BlockSpec((1, 1, N_HEADS, HEAD_DIM, PAGE), lambda i, pt: (layer, pt[i, j], 0, 0, 0))
    lfpage = lambda j: pl.BlockSpec((1, 1, N_HEADS, PAGE), lambda i, pt: (layer, pt[i, j], 0, 0))
    pages = list(range(n_pages))
    grid_spec = pltpu.PrefetchScalarGridSpec(
        num_scalar_prefetch=1,
        grid=(n_s,),
        in_specs=[tok, tok, tok, pl.BlockSpec((1, N_HEADS, 1), lambda i, pt: (i, 0, 0)), tok,
                  pl.BlockSpec(su.shape, lambda i, pt: (0, 0)), pl.BlockSpec(eye.shape, lambda i, pt: (0, 0))]
        + [page(j) for j in pages] + [page(j) for j in pages] + [lfpage(j) for j in pages],
        out_specs=tok,
        scratch_shapes=[pltpu.VMEM((n_pages, N_HEADS, PAGE), F32), pltpu.VMEM((n_pages, N_HEADS, PAGE), F32)])
    return pl.pallas_call(
        _decode_kernel,
        grid_spec=grid_spec,
        out_shape=jax.ShapeDtypeStruct((n_s, N_HEADS, HEAD_DIM), F32),
        compiler_params=_cparams(1),
        name="fox_decode",
    )(page_table, q, k_new, v_new, lf_new, sg, su, eye,
      *([cache_kt] * n_pages), *([cache_vt] * n_pages), *([logf_t] * n_pages))


def _out_mlp_kernel(y_ref, a_ref, wo_ref, g_ref, w1_ref, w2_ref, go_ref, o_ref, *, final_norm):
    y = y_ref[...] + jnp.dot(a_ref[...].astype(BF), wo_ref[...], preferred_element_type=F32)
    hb = _rms(y, g_ref[...]).astype(BF)
    acc = y
    for c in range(D_FF // FF_CHUNK):
        u = jnp.dot(hb, w1_ref[:, c * FF_CHUNK:(c + 1) * FF_CHUNK], preferred_element_type=F32)
        u = jnp.maximum(u, 0.0)
        acc = acc + jnp.dot((u * u).astype(BF), w2_ref[c * FF_CHUNK:(c + 1) * FF_CHUNK, :],
                            preferred_element_type=F32)
    if final_norm:
        acc = _rms(acc, go_ref[...])
    o_ref[...] = acc


def _out_mlp(y, a, wo, g, w1, w2, g_out, final_norm, tm):
    m = y.shape[0]
    row = pl.BlockSpec((tm, D_MODEL), lambda i: (i, 0))
    return pl.pallas_call(
        functools.partial(_out_mlp_kernel, final_norm=final_norm),
        grid=(m // tm,),
        in_specs=[row, row, _const_spec(wo.shape), _const_spec((1, D_MODEL)), _const_spec(w1.shape),
                  _const_spec(w2.shape), _const_spec((1, D_MODEL))],
        out_specs=row,
        out_shape=jax.ShapeDtypeStruct((m, D_MODEL), F32),
        compiler_params=_cparams(1),
        name="out_mlp",
    )(y, a, wo, g, w1, w2, g_out)


def _rwkv_in_kernel(*refs, prev_normed, v_gate, tail):
    if v_gate:
        (y_ref, yp_ref, g_ref, mu_ref, wrkv_ref, w0_ref, w1_ref, w2_ref, a0_ref, a1_ref, a2_ref,
         g1_ref, g2_ref, kk_ref, ka_ref, e1_ref, e2_ref, v0_ref, v1_ref, v2_ref, vf_ref,
         r_out, lw_out, k_out, v_out, kap_out, b_out, g_out, h_out) = refs
    else:
        (y_ref, yp_ref, g_ref, mu_ref, wrkv_ref, w0_ref, w1_ref, w2_ref, a0_ref, a1_ref, a2_ref,
         g1_ref, g2_ref, kk_ref, ka_ref, e1_ref, e2_ref,
         r_out, lw_out, k_out, v_out, kap_out, b_out, g_out, h_out) = refs
    h = _rms(y_ref[...], g_ref[...])
    hp = yp_ref[...] if prev_normed else _rms(yp_ref[...], g_ref[...])
    d = hp - h
    tm = h.shape[0]
    h_out[0] = h[tm - tail:tm, :]
    mix = lambda c: h + d * mu_ref[c:c + 1, :]

    r_out[...] = _dot(mix(0), wrkv_ref[0])
    k = _dot(mix(1), wrkv_ref[1])
    xv = mix(2)
    v = _dot(xv, wrkv_ref[2])
    if v_gate:
        vg = _sigmoid(v0_ref[...] + _dot(_dot(xv, v1_ref[...]), v2_ref[...]))
        v = v + (vf_ref[...] - v) * vg
    v_out[...] = v
    w_pre = w0_ref[...] + _dot(jnp.tanh(_dot(mix(3), w1_ref[...])), w2_ref[...])
    lw_out[...] = -jnp.exp(_log_sigmoid(w_pre) - 0.5)
    a = _sigmoid(a0_ref[...] + _dot(_dot(mix(4), a1_ref[...]), a2_ref[...]))
    g_out[...] = _dot(_sigmoid(_dot(mix(5), g1_ref[...])), g2_ref[...])
    kk = k * kk_ref[...]
    ss = _dot_exact_rhs(kk * kk, e1_ref[...])
    inv = 1.0 / jnp.maximum(jnp.sqrt(ss), L2_EPS)
    kap = kk * _dot_exact_rhs(inv, e2_ref[...])
    kap_out[...] = kap
    b_out[...] = kap * a
    k_out[...] = k * (1.0 + (a - 1.0) * ka_ref[...])


def _rwkv_in(y, yprev, g, p, e1, e2, vgate, v_first, prev_normed, tm, tail):
    m = y.shape[0]
    row = pl.BlockSpec((tm, D_MODEL), lambda i: (i, 0))
    vec = _const_spec((1, D_MODEL))
    args = [y, yprev, g, p["mu"], p["w_rkv"], p["w0"], p["w1"], p["w2"], p["a0"], p["a1"], p["a2"],
            p["g1"], p["g2"], p["k_k"], p["k_a"], e1, e2]
    specs = [row, row, vec, _const_spec(p["mu"].shape), _const_spec(p["w_rkv"].shape), vec,
             _const_spec(p["w1"].shape), _const_spec(p["w2"].shape), vec, _const_spec(p["a1"].shape),
             _const_spec(p["a2"].shape), _const_spec(p["g1"].shape), _const_spec(p["g2"].shape),
             vec, vec, _const_spec(e1.shape), _const_spec(e2.shape)]
    if vgate is not None:
        args += [vgate["v0"], vgate["v1"], vgate["v2"], v_first]
        specs += [vec, _const_spec(vgate["v1"].shape), _const_spec(vgate["v2"].shape), row]
    n_tiles = m // tm
    outs = [jax.ShapeDtypeStruct((m, D_MODEL), F32)] * 7 + [jax.ShapeDtypeStruct((n_tiles, tail, D_MODEL), F32)]
    return pl.pallas_call(
        functools.partial(_rwkv_in_kernel, prev_normed=prev_normed, v_gate=vgate is not None, tail=tail),
        grid=(n_tiles,),
        in_specs=specs,
        out_specs=[row] * 7 + [pl.BlockSpec((1, tail, D_MODEL), lambda i: (i, 0, 0))],
        out_shape=outs,
        compiler_params=_cparams(1),
        name="rwkv_in",
    )(*args)


def _unit_lower_inverse(a_b, eye, dmask):
    ad = a_b * dmask
    low = a_b - ad
    a2 = _bmm(ad, ad)
    a4 = _bmm(a2, a2)
    a8 = _bmm(a4, a4)
    x = eye - ad
    x = x + _bmm(x, a2)
    x = x + _bmm(x, a4)
    dinv = x + _bmm(x, a8)
    n = _bmm(dinv, low)
    n2 = _bmm(n, n)
    y = dinv + _bmm(n2, dinv)
    return y - _bmm(n, y)


def _group_norm_gate(y, r, k, v, g, gnw, gnb, rk, eblk):
    mean = _dot_exact_rhs(y, eblk)
    dev = y - mean
    var = _dot_exact_rhs(dev * dev, eblk)
    yn = dev * lax.rsqrt(var + GN_EPS) * gnw + gnb
    bonus = _dot_exact_rhs(r * k * rk, eblk) * float(HEAD_DIM) * v
    return (yn + bonus) * g


def _rwkv_scan_kernel(r_ref, lw_ref, k_ref, v_ref, kap_ref, b_ref, g_ref, gnw_ref, gnb_ref, rk_ref,
                      tril_ref, amask_ref, dmask_ref, eye_ref, eblk_ref, a_out, st_out, st_ref, y_ref):
    tb = pl.program_id(2)

    @pl.when(tb == 0)
    def _():
        st_ref[...] = jnp.zeros_like(st_ref)

    lane = lax.broadcasted_iota(jnp.int32, (1, 1, LANES), 2)
    m0 = (lane < HEAD_DIM).astype(F32)
    m1 = 1.0 - m0
    stack = lambda x: jnp.concatenate([x * m0, x * m1], axis=1)
    nc = TB // CHUNK
    c = CHUNK
    amask = amask_ref[...]
    ld = lambda ref: jnp.concatenate([ref[0][:, :, q * LANES:(q + 1) * LANES] for q in range(SCAN_PAIRS)], axis=0)
    lw = ld(lw_ref)
    r = ld(r_ref)
    k = ld(k_ref)
    v = ld(v_ref)
    kap = ld(kap_ref)
    bb = ld(b_ref)
    tril = jnp.broadcast_to(tril_ref[...], (SCAN_PAIRS * nc, c, c))
    lw_hi, lw_lo = _split2(lw)
    cl = _bmm(tril, lw_hi) + _bmm(tril, lw_lo)
    cmid = cl[:, c // 2 - 1:c // 2, :]
    clast = cl[:, c - 1:c, :]
    e_r = jnp.exp(cl - cmid)
    e_k = jnp.exp(cmid - cl)
    e_ka = jnp.exp(cl - lw - cmid)
    dc0 = jnp.exp(cmid)
    dc1 = jnp.exp(clast - cmid)
    gam = jnp.exp(clast)
    r_t = r * e_r
    ka_t = kap * e_ka
    k_t = k * e_k
    b_t = bb * e_k
    r_true = stack(r_t * dc0)
    ka_true = stack(ka_t * dc0)
    k_hat = stack(k_t * dc1)
    b_hat = stack(b_t * dc1)
    v_bd = stack(v)
    pmat = jnp.concatenate([stack(ka_t), stack(r_t)], axis=1)
    qmat = jnp.concatenate([stack(k_t), stack(b_t)], axis=1)
    amat = _bmm_nt(pmat, qmat) * amask
    a_k = amat[:, 0:2 * c, 0:2 * c]
    a_b = amat[:, 0:2 * c, 2 * c:4 * c]
    a_rk = amat[:, 2 * c:4 * c, 0:2 * c]
    a_rb = amat[:, 2 * c:4 * c, 2 * c:4 * c]
    tinv = _unit_lower_inverse(a_b, eye_ref[...], dmask_ref[...])
    akv = _bmm(a_k, v_bd)
    wu = _bmm(tinv, jnp.concatenate([ka_true, akv], axis=2))
    w = wu[:, :, 0:2 * c]
    u0 = wu[:, :, 2 * c:4 * c]
    qy = _bmm(a_rb, wu)
    qe = r_true - qy[:, :, 0:2 * c]
    y0 = _bmm(a_rk, v_bd) - qy[:, :, 2 * c:4 * c]
    wtb = _bmm_tn(w, b_hat)
    hm = _bmm_tn(jnp.concatenate([v_bd, u0], axis=1), jnp.concatenate([k_hat, -b_hat], axis=1))

    sts = [st_ref[q] for q in range(SCAN_PAIRS)]
    for ci in range(nc):
        for q in range(SCAN_PAIRS):
            n = q * nc + ci
            y_bd = _dot_nt(qe[n], sts[q]) + y0[n]
            y_ref[n] = y_bd[0:c, :] + y_bd[c:2 * c, :]
            sts[q] = sts[q] * gam[n] - _dot(sts[q], wtb[n]) + hm[n]
    for q in range(SCAN_PAIRS):
        st_ref[q] = sts[q]

    flat = lambda x: x.reshape(TB, LANES)
    for q in range(SCAN_PAIRS):
        rows = slice(q * nc, (q + 1) * nc)
        lanes = slice(q * LANES, (q + 1) * LANES)
        a = _group_norm_gate(flat(y_ref[rows]), flat(r[rows]), flat(k[rows]), flat(v[rows]),
                             flat(g_ref[0, :, :, lanes]), gnw_ref[:, lanes], gnb_ref[:, lanes],
                             rk_ref[:, lanes], eblk_ref[...])
        a_out[0, :, :, lanes] = a.reshape(nc, c, LANES).astype(BF)

    @pl.when(tb == pl.num_programs(2) - 1)
    def _():
        for q in range(SCAN_PAIRS):
            st_out[0, q] = sts[q]


def _rwkv_scan(r, lw, k, v, kap, bb, g, gnw, gnb, rk, consts):
    b, n_chunks = r.shape[0], r.shape[1]
    nc = TB // CHUNK
    blk = pl.BlockSpec((1, nc, CHUNK, SCAN_PAIRS * LANES), lambda i, p, j: (i, j, 0, p))
    vec = pl.BlockSpec((1, SCAN_PAIRS * LANES), lambda i, p, j: (0, p))
    tril, amask, dmask, eye, eblk = consts
    return pl.pallas_call(
        _rwkv_scan_kernel,
        grid=(b, N_PAIRS // SCAN_PAIRS, n_chunks // nc),
        in_specs=[blk] * 7 + [vec] * 3 + [_const_spec(x.shape) for x in consts],
        out_specs=[blk, pl.BlockSpec((1, SCAN_PAIRS, LANES, LANES), lambda i, p, j: (i, p, 0, 0))],
        out_shape=[jax.ShapeDtypeStruct(r.shape, BF),
                   jax.ShapeDtypeStruct((b, N_PAIRS, LANES, LANES), F32)],
        scratch_shapes=[pltpu.VMEM((SCAN_PAIRS, LANES, LANES), F32),
                        pltpu.VMEM((SCAN_PAIRS * nc, CHUNK, LANES), F32)],
        compiler_params=_cparams(3),
        name="rwkv_scan",
    )(r, lw, k, v, kap, bb, g, gnw, gnb, rk, tril, amask, dmask, eye, eblk)


def _rwkv_step_kernel(*refs, aliased):
    s_ref, r_ref, lw_ref, k_ref, v_ref, kap_ref, b_ref, g_ref, gnw_ref, gnb_ref, rk_ref = refs[:11]
    a_out, s_out = refs[12:] if aliased else refs[11:]
    s = s_ref[0, 0]
    r = r_ref[...]
    k = k_ref[...]
    v = v_ref[...]
    s_kk = -jnp.sum(s * kap_ref[...][None], axis=1, keepdims=True)
    s_new = s * jnp.exp(lw_ref[...])[None] + s_kk * b_ref[...][None] + v[:, None, :] * k[None]
    s_out[0, 0] = s_new
    y = jnp.sum(s_new * r[None], axis=1)
    mean = jnp.mean(y, axis=0, keepdims=True)
    dev = y - mean
    var = jnp.mean(dev * dev, axis=0, keepdims=True)
    yn = dev * lax.rsqrt(var + GN_EPS) * gnw_ref[...] + gnb_ref[...]
    bonus = jnp.sum(r * k * rk_ref[...], axis=0, keepdims=True) * v
    a_out[...] = (yn + bonus) * g_ref[...]


def _rwkv_step(state_t, slot, r, lw, k, v, kap, bb, g, gnw, gnb, rk, new_state):
    n_s = state_t.shape[-1]
    st = pl.BlockSpec((1, 1, HEAD_DIM, HEAD_DIM, n_s), lambda h: (slot, h, 0, 0, 0))
    tok = pl.BlockSpec((HEAD_DIM, n_s), lambda h: (h, 0))
    par = pl.BlockSpec((HEAD_DIM, 1), lambda h: (h, 0))
    args = [state_t, r, lw, k, v, kap, bb, g, gnw, gnb, rk]
    in_specs = [st] + [tok] * 7 + [par] * 3
    aliases = {}
    if new_state is not None:
        args.append(new_state)
        in_specs.append(pl.BlockSpec(memory_space=pl.ANY))
        aliases = {11: 1}
    return pl.pallas_call(
        functools.partial(_rwkv_step_kernel, aliased=new_state is not None),
        grid=(N_HEADS,),
        in_specs=in_specs,
        out_specs=[tok, st],
        out_shape=[jax.ShapeDtypeStruct((D_MODEL, n_s), F32),
                   jax.ShapeDtypeStruct(state_t.shape, F32)],
        input_output_aliases=aliases,
        compiler_params=_cparams(1),
        name="rwkv_step",
    )(*args)


def _head_selectors():
    d = jnp.arange(D_MODEL)
    e1 = (d[:, None] // HEAD_DIM == jnp.arange(LANES)[None, :]).astype(BF)
    return e1, e1.T


def _aug_placement():
    h = jnp.arange(LANES)
    col = jnp.arange(D_MODEL)
    base = (h // 2) * LANES + (h % 2) * HEAD_DIM
    valid = (h < N_HEADS)[:, None]
    place = lambda off: (valid & (col[None, :] == (base + off)[:, None])).astype(BF)
    pq = jnp.stack([place(0), place(1), place(2)])
    pk = jnp.stack([place(3), place(4), place(5)])
    within = col % HEAD_DIM
    cq = jnp.where((within >= 3) & (within < 6), -1.0, 0.0).astype(F32)[None, :]
    ck = jnp.where(within < 3, 1.0, 0.0).astype(F32)[None, :]
    return pq, pk, cq, ck


def _scan_constants():
    c = CHUNK
    i = jnp.arange(c)
    tril = (i[None, :] <= i[:, None]).astype(BF)
    r4 = jnp.arange(4 * c)
    same = (r4[:, None] // c) % 2 == (r4[None, :] // c) % 2
    tr = r4[:, None] % c
    ti = r4[None, :] % c
    strict = tr > ti
    incl = tr >= ti
    amask = (same & jnp.where((r4 < 2 * c)[:, None], strict, incl)).astype(F32)
    r2 = jnp.arange(2 * c)
    dmask = (r2[:, None] // 16 == r2[None, :] // 16).astype(F32)
    eye = jnp.eye(2 * c, dtype=F32)
    eblk = ((r2[:, None] // HEAD_DIM == r2[None, :] // HEAD_DIM).astype(F32) / HEAD_DIM).astype(BF)
    return tril, amask, dmask, eye, eblk


def _pad_cols(w, n):
    return jnp.pad(w, ((0, 0), (0, n - w.shape[1])))


def _pad_rows(w, n):
    return jnp.pad(w, ((0, n - w.shape[0]), (0, 0)))


def _heads(x):
    return x.reshape(x.shape[0], N_HEADS, HEAD_DIM)


def kernel(x_prompt, x_sample, cache_k, cache_v, cache_logf, page_table, state_shift, state_wkv,
           norm_mix, norm_mlp, norm_out, fox_w_in, fox_b_f, fox_q_norm, fox_k_norm, fox_w_o,
           rwkv_mu, rwkv_w_rkv, rwkv_w0, rwkv_w1, rwkv_w2, rwkv_a0, rwkv_a1, rwkv_a2,
           rwkv_v0, rwkv_v1, rwkv_v2, rwkv_g1, rwkv_g2, rwkv_k_k, rwkv_k_a, rwkv_r_k,
           rwkv_gn_w, rwkv_gn_b, rwkv_w_o, mlp_w1, mlp_w2):
    b, t, _ = x_prompt.shape
    n_s = x_sample.shape[0]
    depth = norm_mix.shape[0]
    yp = x_prompt.reshape(b * t, D_MODEL)
    ys = x_sample.reshape(n_s, D_MODEL)
    vec = lambda x: x.reshape(1, -1).astype(F32)
    e1, e2 = _head_selectors()
    pq, pk, cq, ck = _aug_placement()
    scan_consts = _scan_constants()
    tri_cum = (jnp.arange(TCUM)[None, :] <= jnp.arange(TCUM)[:, None]).astype(BF)
    tok = jnp.arange(PAGE)
    suffix = (tok[:, None] > tok[None, :]).astype(BF)
    eye64 = jnp.eye(HEAD_DIM, dtype=F32)
    logf_t = jnp.swapaxes(cache_logf, 2, 3)
    cache_kt = jnp.transpose(cache_k, (0, 1, 3, 4, 2))
    cache_vt = jnp.transpose(cache_v, (0, 1, 3, 4, 2))
    state_t = jnp.transpose(state_wkv, (0, 2, 3, 4, 1))

    n_fox = (depth + 1) // 2
    ktvt_p = ktvt_s = None
    lf_p, lf_s = [], []
    sh_p, wkv_p, sh_s = [], [], []
    wkv_s = None
    vf_p = vf_s = None
    for layer in range(depth):
        i = layer // 2
        g_mix = vec(norm_mix[layer])
        if layer % 2 == 0:
            w_in = fox_w_in[i]
            wqkv = w_in[:, :3 * D_MODEL].astype(BF)
            wf_full = _pad_cols(w_in[:, 3 * D_MODEL:3 * D_MODEL + N_HEADS], LANES)
            wf = jnp.stack(_split2(wf_full))
            wg = w_in[:, 3 * D_MODEL + N_HEADS:].astype(BF)
            b_f = _pad_cols(vec(fox_b_f[i]), LANES)
            q_g = vec(jnp.tile(fox_q_norm[i], N_HEADS))
            k_g = vec(jnp.tile(fox_k_norm[i], N_HEADS))
            fox_args = (g_mix, wqkv, wf, wg, b_f, q_g, k_g, e1, e2)
            q, kb, vb, lf, sg, *ktvt_p = _fox_in(yp, *fox_args, tm=TM, q_scale=ATTN_SCALE * LOG2E, seq=t, slot=i,
                                                 n_slots=n_fox, kt_vt=ktvt_p, tokens_out=False)
            qa, ka = _fox_cum(lf.reshape(b, t, LANES), tri_cum, pq, pk, cq, ck)
            shp = lambda x: x.reshape(b, t, D_MODEL)
            mix_p = _flash(shp(q), qa, shp(kb), ka, shp(vb), shp(sg)).reshape(b * t, D_MODEL)
            lf_p.append(lf[:, :N_HEADS].reshape(b, t, N_HEADS))
            q, kb, vb, lf, sg, kt_s, vt_s, k, v = _fox_in(ys, *fox_args, tm=n_s, q_scale=ATTN_SCALE, seq=n_s, slot=i,
                                                          n_slots=n_fox, kt_vt=ktvt_s, tokens_out=True)
            ktvt_s = [kt_s, vt_s]
            lf16 = lf[:, :N_HEADS]
            mix_s = _decode(page_table, i, _heads(q.astype(F32)), _heads(k), _heads(v),
                            lf16[:, :, None], _heads(sg), cache_kt, cache_vt, logf_t, suffix, eye64)
            mix_s = mix_s.reshape(n_s, D_MODEL)
            lf_s.append(lf16.reshape(n_s, 1, N_HEADS))
            w_o = fox_w_o[i].astype(BF)
        else:
            p = {
                "mu": _pad_rows(rwkv_mu[i], 8),
                "w_rkv": rwkv_w_rkv[i].astype(BF),
                "w0": vec(rwkv_w0[i]), "a0": vec(rwkv_a0[i]),
                "w1": _pad_cols(rwkv_w1[i], LANES).astype(BF), "w2": _pad_rows(rwkv_w2[i], LANES).astype(BF),
                "a1": _pad_cols(rwkv_a1[i], LANES).astype(BF), "a2": _pad_rows(rwkv_a2[i], LANES).astype(BF),
                "g1": _pad_cols(rwkv_g1[i], 2 * LANES).astype(BF), "g2": _pad_rows(rwkv_g2[i], 2 * LANES).astype(BF),
                "k_k": vec(rwkv_k_k[i]), "k_a": vec(rwkv_k_a[i]),
            }
            vgate = None
            if i > 0:
                vgate = {"v0": vec(rwkv_v0[i - 1]),
                         "v1": _pad_cols(rwkv_v1[i - 1], LANES).astype(BF),
                         "v2": _pad_rows(rwkv_v2[i - 1], LANES).astype(BF)}
            gnw, gnb, rk = vec(rwkv_gn_w[i]), vec(rwkv_gn_b[i]), vec(rwkv_r_k[i])
            yp3 = yp.reshape(b, t, D_MODEL)
            yprev = jnp.concatenate([jnp.zeros_like(yp3[:, :1]), yp3[:, :-1]], axis=1).reshape(b * t, D_MODEL)
            r, lw, k, v, kap, bb, g, htail = _rwkv_in(yp, yprev, g_mix, p, e1, e2, vgate, vf_p,
                                                      prev_normed=False, tm=TM, tail=8)
            if i == 0:
                vf_p = v
            shp = lambda x: x.reshape(b, t // CHUNK, CHUNK, D_MODEL)
            mix_p, st = _rwkv_scan(shp(r), shp(lw), shp(k), shp(v), shp(kap), shp(bb), shp(g),
                                   gnw, gnb, rk, scan_consts)
            mix_p = mix_p.reshape(b * t, D_MODEL)
            sh_p.append(htail.reshape(b, t // TM, 8, D_MODEL)[:, -1, -1, :])
            st = st.reshape(b, N_PAIRS, 2, HEAD_DIM, 2, HEAD_DIM)
            wkv_p.append(jnp.stack([st[:, :, 0, :, 0, :], st[:, :, 1, :, 1, :]], axis=2)
                         .reshape(b, N_HEADS, HEAD_DIM, HEAD_DIM))
            r, lw, k, v, kap, bb, g, htail = _rwkv_in(ys, state_shift[i], g_mix, p, e1, e2, vgate, vf_s,
                                                      prev_normed=True, tm=n_s, tail=n_s)
            if i == 0:
                vf_s = v
            col = lambda x: x.reshape(D_MODEL, 1)
            mix_s, wkv_s = _rwkv_step(state_t, i, r.T, lw.T, k.T, v.T, kap.T, bb.T, g.T,
                                      col(gnw), col(gnb), col(rk), wkv_s)
            mix_s = mix_s.T
            sh_s.append(htail.reshape(n_s, D_MODEL))
            w_o = rwkv_w_o[i].astype(BF)
        last = layer == depth - 1
        mlp_args = (w_o, vec(norm_mlp[layer]), mlp_w1[layer].astype(BF), mlp_w2[layer].astype(BF),
                    vec(norm_out), last)
        yp = _out_mlp(yp, mix_p, *mlp_args, tm=TM)
        ys = _out_mlp(ys, mix_s, *mlp_args, tm=n_s)
    untr_p = lambda x: jnp.transpose(x.reshape(n_fox, b, N_HEADS, HEAD_DIM, t), (0, 1, 4, 2, 3))
    untr_s = lambda x: jnp.transpose(x.reshape(n_fox, N_HEADS, HEAD_DIM, n_s), (0, 3, 1, 2)).reshape(
        n_fox, n_s, 1, N_HEADS, HEAD_DIM)
    return (yp.reshape(b, t, D_MODEL), ys.reshape(n_s, 1, D_MODEL),
            untr_p(ktvt_p[0]), untr_p(ktvt_p[1]), jnp.stack(lf_p),
            untr_s(ktvt_s[0]), untr_s(ktvt_s[1]), jnp.stack(lf_s),
            jnp.stack(sh_p), jnp.stack(wkv_p), jnp.stack(sh_s), jnp.transpose(wkv_s, (0, 4, 1, 2, 3)))
```

```python
import functools
import math

import jax
import jax.numpy as jnp
from jax import lax
from jax.experimental import pallas as pl
from jax.experimental.pallas import tpu as pltpu

F32 = jnp.float32
BF = jnp.bfloat16

D_MODEL = 1024
HEAD_DIM = 64
N_HEADS = D_MODEL // HEAD_DIM
N_PAIRS = N_HEADS // 2
LANES = 128
D_FF = 4 * D_MODEL
PAGE = 128
ATTN_SCALE = HEAD_DIM ** -0.5
LOG2E = math.log2(math.e)
RMS_EPS = 1e-6
GN_EPS = 64e-5
L2_EPS = 1e-12
NEG = -1e30

TM = 256
TQ = 512
TCUM = 512
CHUNK = 64
TB = 512
SCAN_PAIRS = 2
SLAB = 64
FF_CHUNK = 1024
VMEM_LIMIT = 56 * 1024 * 1024


def _cparams(n_axes):
    return pltpu.CompilerParams(dimension_semantics=("arbitrary",) * n_axes,
                                vmem_limit_bytes=VMEM_LIMIT)


def _const_spec(shape):
    nd = len(shape)
    return pl.BlockSpec(shape, lambda *_: (0,) * nd, pipeline_mode=pl.Buffered(1))


def _dot(a, b):
    return jnp.dot(a.astype(BF), b.astype(BF), preferred_element_type=F32)


def _dot_nt(a, b):
    return lax.dot_general(a.astype(BF), b.astype(BF), (((1,), (1,)), ((), ())),
                           preferred_element_type=F32)


def _bmm(a, b):
    return lax.dot_general(a.astype(BF), b.astype(BF), (((2,), (1,)), ((0,), (0,))),
                           preferred_element_type=F32)


def _bmm_nt(a, b):
    return lax.dot_general(a.astype(BF), b.astype(BF), (((2,), (2,)), ((0,), (0,))),
                           preferred_element_type=F32)


def _bmm_tn(a, b):
    return lax.dot_general(a.astype(BF), b.astype(BF), (((1,), (1,)), ((0,), (0,))),
                           preferred_element_type=F32)


def _split2(x):
    hi = x.astype(BF)
    lo = (x - hi.astype(F32)).astype(BF)
    return hi, lo


def _split3(x):
    hi = x.astype(BF)
    r1 = x - hi.astype(F32)
    mid = r1.astype(BF)
    lo = (r1 - mid.astype(F32)).astype(BF)
    return hi, mid, lo


def _dot_exact_rhs(x, sel):
    hi, lo = _split2(x)
    return (jnp.dot(hi, sel, preferred_element_type=F32)
            + jnp.dot(lo, sel, preferred_element_type=F32))


def _dot_exact_lhs(sel, x):
    hi, lo = _split2(x)
    return (jnp.dot(sel, hi, preferred_element_type=F32)
            + jnp.dot(sel, lo, preferred_element_type=F32))


def _rms(x, g):
    return x * lax.rsqrt(jnp.mean(x * x, axis=-1, keepdims=True) + RMS_EPS) * g


def _log_sigmoid(x):
    return jnp.minimum(x, 0.0) - jnp.log(1.0 + jnp.exp(-jnp.abs(x)))


def _sigmoid(x):
    return 1.0 / (1.0 + jnp.exp(-x))


def _fox_in_kernel(*refs, q_scale, aliased, tokens_out):
    y_ref, g_ref, wqkv_ref, wf_ref, wg_ref, bf_ref, qg_ref, kg_ref, e1_ref, e2_ref = refs[:10]
    outs = refs[12:] if aliased else refs[10:]
    q_ref, kb_ref, vb_ref, lf_ref, sg_ref, kt_ref, vt_ref = outs[:7]
    h = _rms(y_ref[...], g_ref[...])
    hb = h.astype(BF)
    e1 = e1_ref[...]
    e2 = e2_ref[...]

    def headnorm(z, gain):
        ms = _dot_exact_rhs(z * z, e1) * (1.0 / HEAD_DIM)
        inv = lax.rsqrt(ms + RMS_EPS)
        return z * _dot_exact_rhs(inv, e2) * gain

    q = jnp.dot(hb, wqkv_ref[:, 0:D_MODEL], preferred_element_type=F32)
    q_ref[...] = (headnorm(q, qg_ref[...]) * q_scale).astype(BF)
    k = jnp.dot(hb, wqkv_ref[:, D_MODEL:2 * D_MODEL], preferred_element_type=F32)
    k = headnorm(k, kg_ref[...])
    kb_ref[...] = k.astype(BF)
    kt_ref[0, 0] = k.T
    v = jnp.dot(hb, wqkv_ref[:, 2 * D_MODEL:3 * D_MODEL], preferred_element_type=F32)
    vb_ref[...] = v.astype(BF)
    vt_ref[0, 0] = v.T
    if tokens_out:
        outs[7][...] = k
        outs[8][...] = v
    h_hi, h_lo = _split2(h)
    wf_hi = wf_ref[0]
    wf_lo = wf_ref[1]
    f_logit = (jnp.dot(h_hi, wf_hi, preferred_element_type=F32)
               + jnp.dot(h_lo, wf_hi, preferred_element_type=F32)
               + jnp.dot(h_hi, wf_lo, preferred_element_type=F32))
    lane = lax.broadcasted_iota(jnp.int32, f_logit.shape, 1)
    lf_ref[...] = jnp.where(lane < N_HEADS, _log_sigmoid(f_logit + bf_ref[...]), 0.0)
    gate = jnp.dot(hb, wg_ref[...], preferred_element_type=F32)
    sg_ref[...] = _sigmoid(gate)


def _fox_in(y, g, wqkv, wf, wg, b_f, q_g, k_g, e1, e2, *, tm, q_scale, seq, slot, n_slots, kt_vt, tokens_out):
    m = y.shape[0]
    tiles = seq // tm
    row = lambda w: pl.BlockSpec((tm, w), lambda i: (i, 0))
    tposed = pl.BlockSpec((1, 1, D_MODEL, tm), lambda i: (slot, i // tiles, 0, i % tiles))
    tshape = jax.ShapeDtypeStruct((n_slots, m // seq, D_MODEL, seq), F32)
    outs = [jax.ShapeDtypeStruct((m, D_MODEL), BF),
            jax.ShapeDtypeStruct((m, D_MODEL), BF),
            jax.ShapeDtypeStruct((m, D_MODEL), BF),
            jax.ShapeDtypeStruct((m, LANES), F32),
            jax.ShapeDtypeStruct((m, D_MODEL), F32),
            tshape, tshape]
    out_specs = [row(D_MODEL)] * 3 + [row(LANES), row(D_MODEL), tposed, tposed]
    if tokens_out:
        outs += [jax.ShapeDtypeStruct((m, D_MODEL), F32)] * 2
        out_specs += [row(D_MODEL)] * 2
    args = [y, g, wqkv, wf, wg, b_f, q_g, k_g, e1, e2]
    in_specs = [row(D_MODEL), _const_spec((1, D_MODEL)), _const_spec(wqkv.shape),
                _const_spec(wf.shape), _const_spec(wg.shape), _const_spec((1, LANES)),
                _const_spec((1, D_MODEL)), _const_spec((1, D_MODEL)),
                _const_spec(e1.shape), _const_spec(e2.shape)]
    aliases = {}
    if kt_vt is not None:
        args += list(kt_vt)
        in_specs += [pl.BlockSpec(memory_space=pl.ANY)] * 2
        aliases = {10: 5, 11: 6}
    return pl.pallas_call(
        functools.partial(_fox_in_kernel, q_scale=q_scale, aliased=kt_vt is not None, tokens_out=tokens_out),
        grid=(m // tm,),
        in_specs=in_specs,
        out_specs=out_specs,
        out_shape=outs,
        input_output_aliases=aliases,
        compiler_params=_cparams(1),
        name="fox_in",
    )(*args)


def _fox_cum_kernel(lf_ref, tri_ref, pq_ref, pk_ref, cq_ref, ck_ref, qa_ref, ka_ref, carry_ref):
    @pl.when(pl.program_id(1) == 0)
    def _():
        carry_ref[...] = jnp.zeros_like(carry_ref)

    tri = tri_ref[...]
    x = lf_ref[0]
    c = carry_ref[0:1, :]
    for part in _split3(x):
        c = c + jnp.dot(tri, part, preferred_element_type=F32)
    carry_ref[0:1, :] = c[TCUM - 1:TCUM, :]
    qa = cq_ref[...]
    ka = ck_ref[...]
    for j, part in enumerate(_split3(c * LOG2E)):
        qa = qa + jnp.dot(part, pq_ref[j], preferred_element_type=F32)
        ka = ka + jnp.dot(part, pk_ref[j], preferred_element_type=F32)
    qa_ref[0] = qa.astype(BF)
    ka_ref[0] = ka.astype(BF)


def _fox_cum(lf, tri, pq, pk, cq, ck):
    b, t, _ = lf.shape
    blk = lambda w: pl.BlockSpec((1, TCUM, w), lambda i, j: (i, j, 0))
    return pl.pallas_call(
        _fox_cum_kernel,
        grid=(b, t // TCUM),
        in_specs=[blk(LANES), _const_spec(tri.shape), _const_spec(pq.shape), _const_spec(pk.shape),
                  _const_spec(cq.shape), _const_spec(ck.shape)],
        out_specs=[blk(D_MODEL), blk(D_MODEL)],
        out_shape=[jax.ShapeDtypeStruct((b, t, D_MODEL), BF)] * 2,
        scratch_shapes=[pltpu.VMEM((8, LANES), F32)],
        compiler_params=_cparams(2),
        name="fox_cum",
    )(lf, tri, pq, pk, cq, ck)


def _flash_kernel(q_ref, qa_ref, k_ref, ka_ref, v_ref, sg_ref, o_ref,
                  kf_ref, vx_ref, qh_ref, s0_ref, s1_ref, m_ref, acc_ref):
    qi = pl.program_id(2)

    @pl.when(qi == 0)
    def _():
        kf_ref[:, 0:LANES] = k_ref[0]
        kf_ref[:, LANES:2 * LANES] = ka_ref[0]
        vx_ref[:, 0:LANES] = v_ref[0]
        vx_ref[:, LANES:2 * LANES] = jnp.ones((vx_ref.shape[0], LANES), BF)

    lane2 = lax.broadcasted_iota(jnp.int32, (1, 2 * LANES), 1)
    first = (lane2 % LANES) < HEAD_DIM
    qf = jnp.concatenate([q_ref[0], qa_ref[0]], axis=1)
    zero = jnp.zeros_like(qf)
    qh_ref[0:TQ, :] = jnp.where(first, qf, zero)
    qh_ref[TQ:2 * TQ, :] = jnp.where(first, zero, qf)
    m_ref[...] = jnp.full(m_ref.shape, NEG, F32)
    acc_ref[...] = jnp.zeros(acc_ref.shape, F32)
    groups = [(0, TQ), (TQ, 2 * TQ)]

    def scores(kb, s_ref):
        kblk = kf_ref[pl.ds(pl.multiple_of(kb * TQ, TQ), TQ), :]
        for lo, hi in groups:
            s_ref[lo:hi, :] = lax.dot_general(qh_ref[lo:hi, :], kblk, (((1,), (1,)), ((), ())),
                                              preferred_element_type=F32)

    def softmax_pv(kb, s_ref, diagonal):
        vblk = vx_ref[pl.ds(pl.multiple_of(kb * TQ, TQ), TQ), :]
        for lo, hi in groups:
            p_parts, al_parts = [], []
            for r0 in range(lo, hi, SLAB):
                s = s_ref[r0:r0 + SLAB, :]
                if diagonal:
                    row = (r0 % TQ) + lax.broadcasted_iota(jnp.int32, (SLAB, TQ), 0)
                    col = lax.broadcasted_iota(jnp.int32, (SLAB, TQ), 1)
                    s = jnp.where(col <= row, s, NEG)
                m_prev = m_ref[r0:r0 + SLAB, :]
                m_next = jnp.maximum(m_prev, jnp.max(s, axis=1, keepdims=True))
                p_parts.append(jnp.exp2(s - jnp.concatenate([m_next] * (TQ // LANES), axis=1)).astype(BF))
                al_parts.append(jnp.exp2(m_prev - m_next))
                m_ref[r0:r0 + SLAB, :] = m_next
            pv = jnp.dot(jnp.concatenate(p_parts, axis=0), vblk, preferred_element_type=F32)
            alpha = jnp.concatenate(al_parts, axis=0)
            acc_ref[lo:hi, :] = acc_ref[lo:hi, :] * jnp.concatenate([alpha, alpha], axis=1) + pv

    def body(i, carry):
        kb = 2 * i
        softmax_pv(kb, s0_ref, False)
        scores(kb + 1, s1_ref)
        softmax_pv(kb + 1, s1_ref, False)
        scores(kb + 2, s0_ref)
        return carry

    scores(0, s0_ref)
    lax.fori_loop(0, qi // 2, body, 0)

    @pl.when(qi % 2 == 1)
    def _():
        softmax_pv(qi - 1, s0_ref, False)
        scores(qi, s1_ref)
        softmax_pv(qi, s1_ref, True)

    @pl.when(qi % 2 == 0)
    def _():
        softmax_pv(qi, s0_ref, True)

    lane = lax.broadcasted_iota(jnp.int32, (1, LANES), 1)
    o0 = acc_ref[0:TQ, 0:LANES] / acc_ref[0:TQ, LANES:2 * LANES]
    o1 = acc_ref[TQ:2 * TQ, 0:LANES] / acc_ref[TQ:2 * TQ, LANES:2 * LANES]
    o_ref[0] = (jnp.where(lane < HEAD_DIM, o0, o1) * sg_ref[0]).astype(BF)


def _flash(q, qa, kb, ka, vb, sg):
    b, t, _ = q.shape
    qblk = pl.BlockSpec((1, TQ, LANES), lambda i, p, j: (i, j, p))
    kblk = pl.BlockSpec((1, t, LANES), lambda i, p, j: (i, 0, p))
    return pl.pallas_call(
        _flash_kernel,
        grid=(b, N_PAIRS, t // TQ),
        in_specs=[qblk, qblk, kblk, kblk, kblk, qblk],
        out_specs=qblk,
        out_shape=jax.ShapeDtypeStruct((b, t, D_MODEL), BF),
        scratch_shapes=[pltpu.VMEM((t, 2 * LANES), BF),
                        pltpu.VMEM((t, 2 * LANES), BF),
                        pltpu.VMEM((2 * TQ, 2 * LANES), BF),
                        pltpu.VMEM((2 * TQ, TQ), F32),
                        pltpu.VMEM((2 * TQ, TQ), F32),
                        pltpu.VMEM((2 * TQ, LANES), F32),
                        pltpu.VMEM((2 * TQ, 2 * LANES), F32)],
        compiler_params=_cparams(3),
        name="fox_flash",
    )(q, qa, kb, ka, vb, sg)


def _decode_kernel(pt_ref, q_ref, kn_ref, vn_ref, lfn_ref, sg_ref, su_ref, eye_ref, *rest):
    del pt_ref
    n = (len(rest) - 3) // 3
    k_refs, v_refs, lf_refs = rest[0:n], rest[n:2 * n], rest[2 * n:3 * n]
    o_ref, s_ref, p_ref = rest[3 * n:]
    q = q_ref[0]
    eye = eye_ref[...]
    tail = lfn_ref[0]
    for j in reversed(range(n)):
        lf = lf_refs[j][0, 0]
        s_ref[j] = tail + _dot_exact_rhs(lf, su_ref[...])
        tail = tail + jnp.sum(lf, axis=1, keepdims=True)
    for h in range(N_HEADS):
        q_col = jnp.sum(eye * q[h:h + 1, :], axis=1, keepdims=True)
        for j in range(n):
            s_ref[j, h:h + 1, :] += jnp.sum(k_refs[j][0, 0, h] * q_col, axis=0, keepdims=True)
    s = s_ref[...]
    s_new = jnp.sum(q * kn_ref[0], axis=-1, keepdims=True)
    m = jnp.maximum(jnp.max(jnp.max(s, axis=0), axis=1, keepdims=True), s_new)
    p = jnp.exp(s - m[None])
    p_ref[...] = p
    p_new = jnp.exp(s_new - m)
    inv_l = 1.0 / (jnp.sum(jnp.sum(p, axis=0), axis=1, keepdims=True) + p_new)
    new_part = p_new * vn_ref[0]
    scale = inv_l * sg_ref[0]
    for h in range(N_HEADS):
        acc = p_ref[0, h:h + 1, :] * v_refs[0][0, 0, h]
        for j in range(1, n):
            acc = acc + p_ref[j, h:h + 1, :] * v_refs[j][0, 0, h]
        o_col = jnp.sum(acc, axis=1, keepdims=True)
        o_row = jnp.sum(eye * o_col, axis=0, keepdims=True)
        o_ref[0, h:h + 1, :] = (o_row + new_part[h:h + 1, :]) * scale[h:h + 1, :]


def _decode(page_table, layer, q, k_new, v_new, lf_new, sg, cache_kt, cache_vt, logf_t, su, eye):
    n_s, n_pages = page_table.shape
    tok = pl.BlockSpec((1, N_HEADS, HEAD_DIM), lambda i, pt: (i, 0, 0))
    page = lambda j: pl.BlockSpec((1, 1, N_HEADS, HEAD_DIM, PAGE), lambda i, pt: (layer, pt[i, j], 0, 0, 0))
    lfpage = lambda j: pl.BlockSpec((1, 1, N_HEADS, PAGE), lambda i, pt: (layer, pt[i, j], 0, 0))
    pages = list(range(n_pages))
    grid_spec = pltpu.PrefetchScalarGridSpec(
        num_scalar_prefetch=1,
        grid=(n_s,),
        in_specs=[tok, tok, tok, pl.BlockSpec((1, N_HEADS, 1), lambda i, pt: (i, 0, 0)), tok,
                  pl.BlockSpec(su.shape, lambda i, pt: (0, 0)), pl.BlockSpec(eye.shape, lambda i, pt: (0, 0))]
        + [page(j) for j in pages] + [page(j) for j in pages] + [lfpage(j) for j in pages],
        out_specs=tok,
        scratch_shapes=[pltpu.VMEM((n_pages, N_HEADS, PAGE), F32), pltpu.VMEM((n_pages, N_HEADS, PAGE), F32)])
    return pl.pallas_call(
        _decode_kernel,
        grid_spec=grid_spec,
        out_shape=jax.ShapeDtypeStruct((n_s, N_HEADS, HEAD_DIM), F32),
        compiler_params=_cparams(1),
        name="fox_decode",
    )(page_table, q, k_new, v_new, lf_new, sg, su, eye,
      *([cache_kt] * n_pages), *([cache_vt] * n_pages), *([logf_t] * n_pages))


def _out_mlp_kernel(y_ref, a_ref, wo_ref, g_ref, w1_ref, w2_ref, go_ref, o_ref, *, final_norm):
    y = y_ref[...] + jnp.dot(a_ref[...].astype(BF), wo_ref[...], preferred_element_type=F32)
    hb = _rms(y, g_ref[...]).astype(BF)
    acc = y
    for c in range(D_FF // FF_CHUNK):
        u = jnp.dot(hb, w1_ref[:, c * FF_CHUNK:(c + 1) * FF_CHUNK], preferred_element_type=F32)
        u = jnp.maximum(u, 0.0)
        acc = acc + jnp.dot((u * u).astype(BF), w2_ref[c * FF_CHUNK:(c + 1) * FF_CHUNK, :],
                            preferred_element_type=F32)
    if final_norm:
        acc = _rms(acc, go_ref[...])
    o_ref[...] = acc


def _out_mlp(y, a, wo, g, w1, w2, g_out, final_norm, tm):
    m = y.shape[0]
    row = pl.BlockSpec((tm, D_MODEL), lambda i: (i, 0))
    return pl.pallas_call(
        functools.partial(_out_mlp_kernel, final_norm=final_norm),
        grid=(m // tm,),
        in_specs=[row, row, _const_spec(wo.shape), _const_spec((1, D_MODEL)), _const_spec(w1.shape),
                  _const_spec(w2.shape), _const_spec((1, D_MODEL))],
        out_specs=row,
        out_shape=jax.ShapeDtypeStruct((m, D_MODEL), F32),
        compiler_params=_cparams(1),
        name="out_mlp",
    )(y, a, wo, g, w1, w2, g_out)


def _rwkv_in_kernel(*refs, prev_normed, v_gate, tail, seq):
    if v_gate:
        (y_ref, yp_ref, g_ref, mu_ref, wrkv_ref, w0_ref, w1_ref, w2_ref, a0_ref, a1_ref, a2_ref,
         g1_ref, g2_ref, kk_ref, ka_ref, e1_ref, e2_ref, v0_ref, v1_ref, v2_ref, vf_ref,
         r_out, lw_out, k_out, v_out, kap_out, b_out, g_out, h_out) = refs
    else:
        (y_ref, yp_ref, g_ref, mu_ref, wrkv_ref, w0_ref, w1_ref, w2_ref, a0_ref, a1_ref, a2_ref,
         g1_ref, g2_ref, kk_ref, ka_ref, e1_ref, e2_ref,
         r_out, lw_out, k_out, v_out, kap_out, b_out, g_out, h_out) = refs
    h = _rms(y_ref[...], g_ref[...])
    tm = h.shape[0]
    if prev_normed:
        hp = yp_ref[...]
    else:
        prev8 = _rms(yp_ref[...], g_ref[...])
        starts = lax.rem(pl.program_id(0) * tm, seq) == 0
        before = prev8[7:8, :] * jnp.where(starts, 0.0, 1.0)
        row = lax.broadcasted_iota(jnp.int32, h.shape, 0)
        hp = jnp.where(row == 0, before, pltpu.roll(h, 1, axis=0))
    d = hp - h
    h_out[0] = h[tm - tail:tm, :]
    mix = lambda c: h + d * mu_ref[c:c + 1, :]

    r_out[...] = _dot(mix(0), wrkv_ref[0])
    k = _dot(mix(1), wrkv_ref[1])
    xv = mix(2)
    v = _dot(xv, wrkv_ref[2])
    if v_gate:
        vg = _sigmoid(v0_ref[...] + _dot(_dot(xv, v1_ref[...]), v2_ref[...]))
        v = v + (vf_ref[...] - v) * vg
    v_out[...] = v
    w_pre = w0_ref[...] + _dot(jnp.tanh(_dot(mix(3), w1_ref[...])), w2_ref[...])
    lw_out[...] = -jnp.exp(_log_sigmoid(w_pre) - 0.5)
    a = _sigmoid(a0_ref[...] + _dot(_dot(mix(4), a1_ref[...]), a2_ref[...]))
    g_out[...] = _dot(_sigmoid(_dot(mix(5), g1_ref[...])), g2_ref[...])
    kk = k * kk_ref[...]
    ss = _dot_exact_rhs(kk * kk, e1_ref[...])
    inv = 1.0 / jnp.maximum(jnp.sqrt(ss), L2_EPS)
    kap = kk * _dot_exact_rhs(inv, e2_ref[...])
    kap_out[...] = kap
    b_out[...] = kap * a
    k_out[...] = k * (1.0 + (a - 1.0) * ka_ref[...])


def _rwkv_in(y, yprev, g, p, e1, e2, vgate, v_first, prev_normed, tm, tail, seq):
    m = y.shape[0]
    row = pl.BlockSpec((tm, D_MODEL), lambda i: (i, 0))
    prev = row if prev_normed else pl.BlockSpec((8, D_MODEL), lambda i: (jnp.maximum(i * (tm // 8) - 1, 0), 0))
    vec = _const_spec((1, D_MODEL))
    args = [y, yprev, g, p["mu"], p["w_rkv"], p["w0"], p["w1"], p["w2"], p["a0"], p["a1"], p["a2"],
            p["g1"], p["g2"], p["k_k"], p["k_a"], e1, e2]
    specs = [row, prev, vec, _const_spec(p["mu"].shape), _const_spec(p["w_rkv"].shape), vec,
             _const_spec(p["w1"].shape), _const_spec(p["w2"].shape), vec, _const_spec(p["a1"].shape),
             _const_spec(p["a2"].shape), _const_spec(p["g1"].shape), _const_spec(p["g2"].shape),
             vec, vec, _const_spec(e1.shape), _const_spec(e2.shape)]
    if vgate is not None:
        args += [vgate["v0"], vgate["v1"], vgate["v2"], v_first]
        specs += [vec, _const_spec(vgate["v1"].shape), _const_spec(vgate["v2"].shape), row]
    n_tiles = m // tm
    outs = [jax.ShapeDtypeStruct((m, D_MODEL), F32)] * 7 + [jax.ShapeDtypeStruct((n_tiles, tail, D_MODEL), F32)]
    return pl.pallas_call(
        functools.partial(_rwkv_in_kernel, prev_normed=prev_normed, v_gate=vgate is not None, tail=tail, seq=seq),
        grid=(n_tiles,),
        in_specs=specs,
        out_specs=[row] * 7 + [pl.BlockSpec((1, tail, D_MODEL), lambda i: (i, 0, 0))],
        out_shape=outs,
        compiler_params=_cparams(1),
        name="rwkv_in",
    )(*args)


def _unit_lower_inverse(a_b, eye, dmask):
    ad = a_b * dmask
    low = a_b - ad
    a2 = _bmm(ad, ad)
    a4 = _bmm(a2, a2)
    a8 = _bmm(a4, a4)
    x = eye - ad
    x = x + _bmm(x, a2)
    x = x + _bmm(x, a4)
    dinv = x + _bmm(x, a8)
    n = _bmm(dinv, low)
    n2 = _bmm(n, n)
    y = dinv + _bmm(n2, dinv)
    return y - _bmm(n, y)


def _group_norm_gate(y, r, k, v, g, gnw, gnb, rk, eblk):
    mean = _dot_exact_rhs(y, eblk)
    dev = y - mean
    var = _dot_exact_rhs(dev * dev, eblk)
    yn = dev * lax.rsqrt(var + GN_EPS) * gnw + gnb
    bonus = _dot_exact_rhs(r * k * rk, eblk) * float(HEAD_DIM) * v
    return (yn + bonus) * g


def _rwkv_scan_kernel(r_ref, lw_ref, k_ref, v_ref, kap_ref, b_ref, g_ref, gnw_ref, gnb_ref, rk_ref,
                      tril_ref, amask_ref, dmask_ref, eye_ref, eblk_ref, a_out, st_out, st_ref, y_ref):
    tb = pl.program_id(2)

    @pl.when(tb == 0)
    def _():
        st_ref[...] = jnp.zeros_like(st_ref)

    lane = lax.broadcasted_iota(jnp.int32, (1, 1, LANES), 2)
    m0 = (lane < HEAD_DIM).astype(F32)
    m1 = 1.0 - m0
    stack = lambda x: jnp.concatenate([x * m0, x * m1], axis=1)
    nc = TB // CHUNK
    c = CHUNK
    amask = amask_ref[...]
    ld = lambda ref: jnp.concatenate([ref[0][:, :, q * LANES:(q + 1) * LANES] for q in range(SCAN_PAIRS)], axis=0)
    lw = ld(lw_ref)
    r = ld(r_ref)
    k = ld(k_ref)
    v = ld(v_ref)
    kap = ld(kap_ref)
    bb = ld(b_ref)
    tril = jnp.broadcast_to(tril_ref[...], (SCAN_PAIRS * nc, c, c))
    lw_hi, lw_lo = _split2(lw)
    cl = _bmm(tril, lw_hi) + _bmm(tril, lw_lo)
    cmid = cl[:, c // 2 - 1:c // 2, :]
    clast = cl[:, c - 1:c, :]
    e_r = jnp.exp(cl - cmid)
    e_k = jnp.exp(cmid - cl)
    e_ka = jnp.exp(cl - lw - cmid)
    dc0 = jnp.exp(cmid)
    dc1 = jnp.exp(clast - cmid)
    gam = jnp.exp(clast)
    r_t = r * e_r
    ka_t = kap * e_ka
    k_t = k * e_k
    b_t = bb * e_k
    r_true = stack(r_t * dc0)
    ka_true = stack(ka_t * dc0)
    k_hat = stack(k_t * dc1)
    b_hat = stack(b_t * dc1)
    v_bd = stack(v)
    pmat = jnp.concatenate([stack(ka_t), stack(r_t)], axis=1)
    qmat = jnp.concatenate([stack(k_t), stack(b_t)], axis=1)
    amat = _bmm_nt(pmat, qmat) * amask
    a_k = amat[:, 0:2 * c, 0:2 * c]
    a_b = amat[:, 0:2 * c, 2 * c:4 * c]
    a_rk = amat[:, 2 * c:4 * c, 0:2 * c]
    a_rb = amat[:, 2 * c:4 * c, 2 * c:4 * c]
    tinv = _unit_lower_inverse(a_b, eye_ref[...], dmask_ref[...])
    akv = _bmm(a_k, v_bd)
    wu = _bmm(tinv, jnp.concatenate([ka_true, akv], axis=2))
    w = wu[:, :, 0:2 * c]
    u0 = wu[:, :, 2 * c:4 * c]
    qy = _bmm(a_rb, wu)
    qe = r_true - qy[:, :, 0:2 * c]
    y0 = _bmm(a_rk, v_bd) - qy[:, :, 2 * c:4 * c]
    wtb = _bmm_tn(w, b_hat)
    hm = _bmm_tn(jnp.concatenate([v_bd, u0], axis=1), jnp.concatenate([k_hat, -b_hat], axis=1))

    sts = [st_ref[q] for q in range(SCAN_PAIRS)]
    for ci in range(nc):
        for q in range(SCAN_PAIRS):
            n = q * nc + ci
            y_bd = _dot_nt(qe[n], sts[q]) + y0[n]
            y_ref[n] = y_bd[0:c, :] + y_bd[c:2 * c, :]
            sts[q] = sts[q] * gam[n] - _dot(sts[q], wtb[n]) + hm[n]
    for q in range(SCAN_PAIRS):
        st_ref[q] = sts[q]

    flat = lambda x: x.reshape(TB, LANES)
    for q in range(SCAN_PAIRS):
        rows = slice(q * nc, (q + 1) * nc)
        lanes = slice(q * LANES, (q + 1) * LANES)
        a = _group_norm_gate(flat(y_ref[rows]), flat(r[rows]), flat(k[rows]), flat(v[rows]),
                             flat(g_ref[0, :, :, lanes]), gnw_ref[:, lanes], gnb_ref[:, lanes],
                             rk_ref[:, lanes], eblk_ref[...])
        a_out[0, :, :, lanes] = a.reshape(nc, c, LANES).astype(BF)

    @pl.when(tb == pl.num_programs(2) - 1)
    def _():
        for q in range(SCAN_PAIRS):
            st_out[0, q] = sts[q]


def _rwkv_scan(r, lw, k, v, kap, bb, g, gnw, gnb, rk, consts):
    b, n_chunks = r.shape[0], r.shape[1]
    nc = TB // CHUNK
    blk = pl.BlockSpec((1, nc, CHUNK, SCAN_PAIRS * LANES), lambda i, p, j: (i, j, 0, p))
    vec = pl.BlockSpec((1, SCAN_PAIRS * LANES), lambda i, p, j: (0, p))
    tril, amask, dmask, eye, eblk = consts
    return pl.pallas_call(
        _rwkv_scan_kernel,
        grid=(b, N_PAIRS // SCAN_PAIRS, n_chunks // nc),
        in_specs=[blk] * 7 + [vec] * 3 + [_const_spec(x.shape) for x in consts],
        out_specs=[blk, pl.BlockSpec((1, SCAN_PAIRS, LANES, LANES), lambda i, p, j: (i, p, 0, 0))],
        out_shape=[jax.ShapeDtypeStruct(r.shape, BF),
                   jax.ShapeDtypeStruct((b, N_PAIRS, LANES, LANES), F32)],
        scratch_shapes=[pltpu.VMEM((SCAN_PAIRS, LANES, LANES), F32),
                        pltpu.VMEM((SCAN_PAIRS * nc, CHUNK, LANES), F32)],
        compiler_params=_cparams(3),
        name="rwkv_scan",
    )(r, lw, k, v, kap, bb, g, gnw, gnb, rk, tril, amask, dmask, eye, eblk)


def _rwkv_step_kernel(*refs, aliased):
    s_ref, r_ref, lw_ref, k_ref, v_ref, kap_ref, b_ref, g_ref, gnw_ref, gnb_ref, rk_ref = refs[:11]
    a_out, s_out = refs[12:] if aliased else refs[11:]
    s = s_ref[0, 0]
    r = r_ref[...]
    k = k_ref[...]
    v = v_ref[...]
    s_kk = -jnp.sum(s * kap_ref[...][None], axis=1, keepdims=True)
    s_new = s * jnp.exp(lw_ref[...])[None] + s_kk * b_ref[...][None] + v[:, None, :] * k[None]
    s_out[0, 0] = s_new
    y = jnp.sum(s_new * r[None], axis=1)
    mean = jnp.mean(y, axis=0, keepdims=True)
    dev = y - mean
    var = jnp.mean(dev * dev, axis=0, keepdims=True)
    yn = dev * lax.rsqrt(var + GN_EPS) * gnw_ref[...] + gnb_ref[...]
    bonus = jnp.sum(r * k * rk_ref[...], axis=0, keepdims=True) * v
    a_out[...] = (yn + bonus) * g_ref[...]


def _rwkv_step(state_t, slot, r, lw, k, v, kap, bb, g, gnw, gnb, rk, new_state):
    n_s = state_t.shape[-1]
    st = pl.BlockSpec((1, 1, HEAD_DIM, HEAD_DIM, n_s), lambda h: (slot, h, 0, 0, 0))
    tok = pl.BlockSpec((HEAD_DIM, n_s), lambda h: (h, 0))
    par = pl.BlockSpec((HEAD_DIM, 1), lambda h: (h, 0))
    args = [state_t, r, lw, k, v, kap, bb, g, gnw, gnb, rk]
    in_specs = [st] + [tok] * 7 + [par] * 3
    aliases = {}
    if new_state is not None:
        args.append(new_state)
        in_specs.append(pl.BlockSpec(memory_space=pl.ANY))
        aliases = {11: 1}
    return pl.pallas_call(
        functools.partial(_rwkv_step_kernel, aliased=new_state is not None),
        grid=(N_HEADS,),
        in_specs=in_specs,
        out_specs=[tok, st],
        out_shape=[jax.ShapeDtypeStruct((D_MODEL, n_s), F32),
                   jax.ShapeDtypeStruct(state_t.shape, F32)],
        input_output_aliases=aliases,
        compiler_params=_cparams(1),
        name="rwkv_step",
    )(*args)


def _head_selectors():
    d = jnp.arange(D_MODEL)
    e1 = (d[:, None] // HEAD_DIM == jnp.arange(LANES)[None, :]).astype(BF)
    return e1, e1.T


def _aug_placement():
    h = jnp.arange(LANES)
    col = jnp.arange(D_MODEL)
    base = (h // 2) * LANES + (h % 2) * HEAD_DIM
    valid = (h < N_HEADS)[:, None]
    place = lambda off: (valid & (col[None, :] == (base + off)[:, None])).astype(BF)
    pq = jnp.stack([place(0), place(1), place(2)])
    pk = jnp.stack([place(3), place(4), place(5)])
    within = col % HEAD_DIM
    cq = jnp.where((within >= 3) & (within < 6), -1.0, 0.0).astype(F32)[None, :]
    ck = jnp.where(within < 3, 1.0, 0.0).astype(F32)[None, :]
    return pq, pk, cq, ck


def _scan_constants():
    c = CHUNK
    i = jnp.arange(c)
    tril = (i[None, :] <= i[:, None]).astype(BF)
    r4 = jnp.arange(4 * c)
    same = (r4[:, None] // c) % 2 == (r4[None, :] // c) % 2
    tr = r4[:, None] % c
    ti = r4[None, :] % c
    strict = tr > ti
    incl = tr >= ti
    amask = (same & jnp.where((r4 < 2 * c)[:, None], strict, incl)).astype(F32)
    r2 = jnp.arange(2 * c)
    dmask = (r2[:, None] // 16 == r2[None, :] // 16).astype(F32)
    eye = jnp.eye(2 * c, dtype=F32)
    eblk = ((r2[:, None] // HEAD_DIM == r2[None, :] // HEAD_DIM).astype(F32) / HEAD_DIM).astype(BF)
    return tril, amask, dmask, eye, eblk


def _pad_cols(w, n):
    return jnp.pad(w, ((0, 0), (0, n - w.shape[1])))


def _pad_rows(w, n):
    return jnp.pad(w, ((0, n - w.shape[0]), (0, 0)))


def _heads(x):
    return x.reshape(x.shape[0], N_HEADS, HEAD_DIM)


def kernel(x_prompt, x_sample, cache_k, cache_v, cache_logf, page_table, state_shift, state_wkv,
           norm_mix, norm_mlp, norm_out, fox_w_in, fox_b_f, fox_q_norm, fox_k_norm, fox_w_o,
           rwkv_mu, rwkv_w_rkv, rwkv_w0, rwkv_w1, rwkv_w2, rwkv_a0, rwkv_a1, rwkv_a2,
           rwkv_v0, rwkv_v1, rwkv_v2, rwkv_g1, rwkv_g2, rwkv_k_k, rwkv_k_a, rwkv_r_k,
           rwkv_gn_w, rwkv_gn_b, rwkv_w_o, mlp_w1, mlp_w2):
    b, t, _ = x_prompt.shape
    n_s = x_sample.shape[0]
    depth = norm_mix.shape[0]
    yp = x_prompt.reshape(b * t, D_MODEL)
    ys = x_sample.reshape(n_s, D_MODEL)
    vec = lambda x: x.reshape(1, -1).astype(F32)
    e1, e2 = _head_selectors()
    pq, pk, cq, ck = _aug_placement()
    scan_consts = _scan_constants()
    tri_cum = (jnp.arange(TCUM)[None, :] <= jnp.arange(TCUM)[:, None]).astype(BF)
    tok = jnp.arange(PAGE)
    suffix = (tok[:, None] > tok[None, :]).astype(BF)
    eye64 = jnp.eye(HEAD_DIM, dtype=F32)
    logf_t = jnp.swapaxes(cache_logf, 2, 3)
    cache_kt = jnp.transpose(cache_k, (0, 1, 3, 4, 2))
    cache_vt = jnp.transpose(cache_v, (0, 1, 3, 4, 2))
    state_t = jnp.transpose(state_wkv, (0, 2, 3, 4, 1))

    n_fox = (depth + 1) // 2
    ktvt_p = ktvt_s = None
    lf_p, lf_s = [], []
    sh_p, wkv_p, sh_s = [], [], []
    wkv_s = None
    vf_p = vf_s = None
    for layer in range(depth):
        i = layer // 2
        g_mix = vec(norm_mix[layer])
        if layer % 2 == 0:
            w_in = fox_w_in[i]
            wqkv = w_in[:, :3 * D_MODEL].astype(BF)
            wf_full = _pad_cols(w_in[:, 3 * D_MODEL:3 * D_MODEL + N_HEADS], LANES)
            wf = jnp.stack(_split2(wf_full))
            wg = w_in[:, 3 * D_MODEL + N_HEADS:].astype(BF)
            b_f = _pad_cols(vec(fox_b_f[i]), LANES)
            q_g = vec(jnp.tile(fox_q_norm[i], N_HEADS))
            k_g = vec(jnp.tile(fox_k_norm[i], N_HEADS))
            fox_args = (g_mix, wqkv, wf, wg, b_f, q_g, k_g, e1, e2)
            q, kb, vb, lf, sg, *ktvt_p = _fox_in(yp, *fox_args, tm=TM, q_scale=ATTN_SCALE * LOG2E, seq=t, slot=i,
                                                 n_slots=n_fox, kt_vt=ktvt_p, tokens_out=False)
            qa, ka = _fox_cum(lf.reshape(b, t, LANES), tri_cum, pq, pk, cq, ck)
            shp = lambda x: x.reshape(b, t, D_MODEL)
            mix_p = _flash(shp(q), qa, shp(kb), ka, shp(vb), shp(sg)).reshape(b * t, D_MODEL)
            lf_p.append(lf[:, :N_HEADS].reshape(b, t, N_HEADS))
            q, kb, vb, lf, sg, kt_s, vt_s, k, v = _fox_in(ys, *fox_args, tm=n_s, q_scale=ATTN_SCALE, seq=n_s, slot=i,
                                                          n_slots=n_fox, kt_vt=ktvt_s, tokens_out=True)
            ktvt_s = [kt_s, vt_s]
            lf16 = lf[:, :N_HEADS]
            mix_s = _decode(page_table, i, _heads(q.astype(F32)), _heads(k), _heads(v),
                            lf16[:, :, None], _heads(sg), cache_kt, cache_vt, logf_t, suffix, eye64)
            mix_s = mix_s.reshape(n_s, D_MODEL)
            lf_s.append(lf16.reshape(n_s, 1, N_HEADS))
            w_o = fox_w_o[i].astype(BF)
        else:
            p = {
                "mu": _pad_rows(rwkv_mu[i], 8),
                "w_rkv": rwkv_w_rkv[i].astype(BF),
                "w0": vec(rwkv_w0[i]), "a0": vec(rwkv_a0[i]),
                "w1": _pad_cols(rwkv_w1[i], LANES).astype(BF), "w2": _pad_rows(rwkv_w2[i], LANES).astype(BF),
                "a1": _pad_cols(rwkv_a1[i], LANES).astype(BF), "a2": _pad_rows(rwkv_a2[i], LANES).astype(BF),
                "g1": _pad_cols(rwkv_g1[i], 2 * LANES).astype(BF), "g2": _pad_rows(rwkv_g2[i], 2 * LANES).astype(BF),
                "k_k": vec(rwkv_k_k[i]), "k_a": vec(rwkv_k_a[i]),
            }
            vgate = None
            if i > 0:
                vgate = {"v0": vec(rwkv_v0[i - 1]),
                         "v1": _pad_cols(rwkv_v1[i - 1], LANES).astype(BF),
                         "v2": _pad_rows(rwkv_v2[i - 1], LANES).astype(BF)}
            gnw, gnb, rk = vec(rwkv_gn_w[i]), vec(rwkv_gn_b[i]), vec(rwkv_r_k[i])
            r, lw, k, v, kap, bb, g, htail = _rwkv_in(yp, yp, g_mix, p, e1, e2, vgate, vf_p,
                                                      prev_normed=False, tm=TM, tail=8, seq=t)
            if i == 0:
                vf_p = v
            shp = lambda x: x.reshape(b, t // CHUNK, CHUNK, D_MODEL)
            mix_p, st = _rwkv_scan(shp(r), shp(lw), shp(k), shp(v), shp(kap), shp(bb), shp(g),
                                   gnw, gnb, rk, scan_consts)
            mix_p = mix_p.reshape(b * t, D_MODEL)
            sh_p.append(htail.reshape(b, t // TM, 8, D_MODEL)[:, -1, -1, :])
            st = st.reshape(b, N_PAIRS, 2, HEAD_DIM, 2, HEAD_DIM)
            wkv_p.append(jnp.stack([st[:, :, 0, :, 0, :], st[:, :, 1, :, 1, :]], axis=2)
                         .reshape(b, N_HEADS, HEAD_DIM, HEAD_DIM))
            r, lw, k, v, kap, bb, g, htail = _rwkv_in(ys, state_shift[i], g_mix, p, e1, e2, vgate, vf_s,
                                                      prev_normed=True, tm=n_s, tail=n_s, seq=n_s)
            if i == 0:
                vf_s = v
            col = lambda x: x.reshape(D_MODEL, 1)
            mix_s, wkv_s = _rwkv_step(state_t, i, r.T, lw.T, k.T, v.T, kap.T, bb.T, g.T,
                                      col(gnw), col(gnb), col(rk), wkv_s)
            mix_s = mix_s.T
            sh_s.append(htail.reshape(n_s, D_MODEL))
            w_o = rwkv_w_o[i].astype(BF)
        last = layer == depth - 1
        mlp_args = (w_o, vec(norm_mlp[layer]), mlp_w1[layer].astype(BF), mlp_w2[layer].astype(BF),
                    vec(norm_out), last)
        yp = _out_mlp(yp, mix_p, *mlp_args, tm=TM)
        ys = _out_mlp(ys, mix_s, *mlp_args, tm=n_s)
    untr_p = lambda x: jnp.transpose(x.reshape(n_fox, b, N_HEADS, HEAD_DIM, t), (0, 1, 4, 2, 3))
    untr_s = lambda x: jnp.transpose(x.reshape(n_fox, N_HEADS, HEAD_DIM, n_s), (0, 3, 1, 2)).reshape(
        n_fox, n_s, 1, N_HEADS, HEAD_DIM)
    return (yp.reshape(b, t, D_MODEL), ys.reshape(n_s, 1, D_MODEL),
            untr_p(ktvt_p[0]), untr_p(ktvt_p[1]), jnp.stack(lf_p),
            untr_s(ktvt_s[0]), untr_s(ktvt_s[1]), jnp.stack(lf_s),
            jnp.stack(sh_p), jnp.stack(wkv_p), jnp.stack(sh_s), jnp.transpose(wkv_s, (0, 4, 1, 2, 3)))
```
